```python
import jax, jax.numpy as jnp
from jax import lax
import numpy as np

D_MODEL = 4096
BATCH = 2
SEQ = 4096
DEPTH = 4

CHUNK = 64
Q_BLOCK = 128
PLE_DIM = 256
N_EVEN = (DEPTH + 1) // 2
N_ODD = DEPTH // 2
MIX_WIDTH = D_MODEL
GROUP_WIDTH = MIX_WIDTH // 2
CONV_CH = GROUP_WIDTH
CONV_WIDTH = 31
FOX_HEADS = 16
FOX_HD = GROUP_WIDTH // FOX_HEADS
RET_HEADS = 8
RET_HD = GROUP_WIDTH // RET_HEADS
SB_HEADS = 16
SB_HD = GROUP_WIDTH // SB_HEADS
EVEN_IN = 2 * CONV_CH + 3 * GROUP_WIDTH + FOX_HEADS
ODD_IN = 4 * GROUP_WIDTH + 3 * GROUP_WIDTH
N_EXPERTS = 16
N_GROUPS = 4
EXPERTS_PER_GROUP = N_EXPERTS // N_GROUPS
TOP_K = 2
D_EXPERT = D_MODEL // 8
DEEPNORM_ALPHA = (2 * DEPTH) ** 0.25
DEEPNORM_BETA = (8 * DEPTH) ** -0.25
ROPE_BASE = 10000.0
LN_EPS = 1e-5

kernel_name = "hybrid_streaming_conv_fox_retention_stickbreaking_moe"


def _layernorm(x, g, b):
    xf = x.astype(jnp.float32)
    mu = jnp.mean(xf, axis=-1, keepdims=True)
    var = jnp.mean(jnp.square(xf - mu), axis=-1, keepdims=True)
    y = (xf - mu) * lax.rsqrt(var + LN_EPS)
    return (y * g.astype(jnp.float32) + b.astype(jnp.float32)).astype(x.dtype)


def _heads(t, n_heads):
    b, s, w = t.shape
    return t.reshape(b, s, n_heads, w // n_heads)


def _to_blocks(t):
    b, s, h, d = t.shape
    return t.reshape(b, s // Q_BLOCK, Q_BLOCK, h, d).transpose(1, 0, 3, 2, 4)


def _from_blocks(t):
    nb, b, h, qb, d = t.shape
    return t.transpose(1, 0, 3, 2, 4).reshape(b, nb * qb, h * d)


def _conformer_conv(u, conv_w, conv_b, ln_g, ln_b):
    a, gate = jnp.split(u, 2, axis=-1)
    y = a * jax.nn.sigmoid(gate)
    y = lax.conv_general_dilated(
        y, conv_w[:, None, :].astype(y.dtype), window_strides=(1,),
        padding=((CONV_WIDTH - 1, 0),),
        dimension_numbers=('NWC', 'WIO', 'NWC'), feature_group_count=CONV_CH)
    y = y + conv_b
    y = _layernorm(y, ln_g, ln_b)
    return jax.nn.silu(y)


def _forgetting_attention(q, k, v, f_logit):
    b, s, h, hd = q.shape
    scale = hd ** -0.5
    cum = jnp.cumsum(jax.nn.log_sigmoid(f_logit.astype(jnp.float32)), axis=1).transpose(0, 2, 1)
    kf = k.astype(jnp.float32).transpose(0, 2, 1, 3)
    vt = v.transpose(0, 2, 1, 3)
    key_pos = jnp.arange(s)
    qb = _to_blocks(q)
    cb = cum.reshape(b, h, s // Q_BLOCK, Q_BLOCK).transpose(2, 0, 1, 3)
    idx = jnp.arange(s // Q_BLOCK)

    def block(args):
        qi, ci, i = args
        logits = (jnp.einsum('bhqd,bhkd->bhqk', qi.astype(jnp.float32), kf) * scale
                  + ci[..., :, None] - cum[..., None, :])
        q_pos = i * Q_BLOCK + jnp.arange(Q_BLOCK)
        mask = key_pos[None, :] <= q_pos[:, None]
        w = jax.nn.softmax(jnp.where(mask, logits, -jnp.inf), axis=-1)
        return jnp.einsum('bhqk,bhkd->bhqd', w.astype(v.dtype), vt)

    return _from_blocks(lax.map(block, (qb, cb, idx)))


def _stick_breaking_attention(q, k, v):
    b, s, h, hd = q.shape
    scale = hd ** -0.5
    kf = k.astype(jnp.float32).transpose(0, 2, 1, 3)
    vt = v.transpose(0, 2, 1, 3)
    key_pos = jnp.arange(s)
    qb = _to_blocks(q)
    idx = jnp.arange(s // Q_BLOCK)

    def block(args):
        qi, i = args
        z = jnp.einsum('bhqd,bhkd->bhqk', qi.astype(jnp.float32), kf) * scale
        q_pos = i * Q_BLOCK + jnp.arange(Q_BLOCK)
        mask = key_pos[None, :] < q_pos[:, None]
        log_1m = jnp.where(mask, jax.nn.log_sigmoid(-z), 0.0)
        later = lax.cumsum(log_1m, axis=3, reverse=True) - log_1m
        w = jnp.where(mask, jnp.exp(jax.nn.log_sigmoid(z) + later), 0.0)
        return jnp.einsum('bhqk,bhkd->bhqd', w.astype(v.dtype), vt)

    return _from_blocks(lax.map(block, (qb, idx)))


def _rotary(t, positions):
    d = t.shape[-1]
    inv = ROPE_BASE ** (-jnp.arange(0, d, 2, dtype=jnp.float32) / d)
    ang = positions.astype(jnp.float32)[..., None] * inv
    cos = jnp.cos(ang)[:, :, None, :]
    sin = jnp.sin(ang)[:, :, None, :]
    tf = t.astype(jnp.float32)
    t1, t2 = tf[..., 0::2], tf[..., 1::2]
    return jnp.stack([t1 * cos - t2 * sin, t1 * sin + t2 * cos], axis=-1).reshape(t.shape)


def _retention(q, k, v, positions):
    b, s, h, d = q.shape
    nc = s // CHUNK
    log_g = jnp.log1p(-jnp.exp2(-5.0 - jnp.arange(h, dtype=jnp.float32)))
    qc = (_rotary(q, positions) * d ** -0.5).reshape(b, nc, CHUNK, h, d)
    kc = _rotary(k, positions).reshape(b, nc, CHUNK, h, d)
    vc = v.astype(jnp.float32).reshape(b, nc, CHUNK, h, d)
    n = jnp.arange(CHUNK, dtype=jnp.float32)
    intra_decay = jnp.exp(jnp.abs(n[:, None] - n[None, :])[None] * log_g[:, None, None])
    scores = jnp.einsum('bclhd,bcmhd->bchlm', qc, kc) * intra_decay
    intra = jnp.einsum('bchlm,bcmhe->bclhe', scores, vc)
    k_decay = jnp.exp((CHUNK - 1 - n)[:, None] * log_g[None, :])
    kv = jnp.einsum('bcmhd,bcmhe->cbhde', kc * k_decay[:, :, None], vc)
    chunk_decay = jnp.exp(CHUNK * log_g)[None, :, None, None]

    def step(state, kv_c):
        return chunk_decay * state + kv_c, state

    _, prev = lax.scan(step, jnp.zeros((b, h, d, d), jnp.float32), kv)
    q_decay = jnp.exp((n + 1.0)[:, None] * log_g[None, :])
    inter = jnp.einsum('bclhd,cbhde->bclhe', qc * q_decay[:, :, None], prev)
    return (intra + inter).reshape(b, s, h, d)


def _head_groupnorm(y, g):
    b, s, h, d = y.shape
    mu = jnp.mean(y, axis=-1, keepdims=True)
    var = jnp.mean(jnp.square(y - mu), axis=-1, keepdims=True)
    return ((y - mu) * lax.rsqrt(var + LN_EPS)).reshape(b, s, h * d) * g.astype(jnp.float32)


def _moe(x, w_router, b_router, w_gate, w_up, w_down):
    b, s, d = x.shape
    xt = x.reshape(b * s, d)
    logits = (xt @ w_router).astype(jnp.float32) + b_router.astype(jnp.float32)
    probs = jax.nn.softmax(logits, axis=-1)
    grouped = probs.reshape(-1, N_GROUPS, EXPERTS_PER_GROUP)
    g_sel = jnp.argmax(jnp.max(grouped, axis=-1), axis=-1)
    in_group = jnp.take_along_axis(grouped, g_sel[:, None, None], axis=1)[:, 0]
    top_w, top_local = lax.top_k(in_group, TOP_K)
    top_w = top_w / jnp.sum(top_w, axis=-1, keepdims=True)
    top_e = g_sel[:, None] * EXPERTS_PER_GROUP + top_local
    combine = jnp.sum(jax.nn.one_hot(top_e, N_EXPERTS, dtype=jnp.float32) * top_w[..., None], axis=1)
    hg = jnp.einsum('nd,edf->nef', xt, w_gate)
    hu = jnp.einsum('nd,edf->nef', xt, w_up)
    hidden = jax.nn.silu(hg) * hu * combine[..., None].astype(x.dtype)
    return jnp.einsum('nef,efd->nd', hidden, w_down).reshape(b, s, d)


def setup_inputs(seed: int = 0) -> dict:
    key = jax.random.key(seed)
    ks = jax.random.split(key, 26)

    def nrm(k, shape, scale):
        return jax.random.normal(k, shape, jnp.float32) * scale

    x = nrm(ks[0], (BATCH, SEQ, D_MODEL), 1.0)
    p = nrm(ks[1], (DEPTH, BATCH, SEQ, PLE_DIM), 1.0)
    offset = jax.random.randint(ks[2], (BATCH, 1), 0, 64, dtype=jnp.int32) * CHUNK
    positions = (offset + jnp.arange(SEQ, dtype=jnp.int32)[None, :]).astype(jnp.int32)
    return {
        "x": x,
        "p": p,
        "positions": positions,
        "w_in_even": nrm(ks[3], (N_EVEN, D_MODEL, EVEN_IN), D_MODEL ** -0.5),
        "conv_w": nrm(ks[4], (N_EVEN, CONV_WIDTH, CONV_CH), CONV_WIDTH ** -0.5),
        "conv_b": nrm(ks[5], (N_EVEN, CONV_CH), 0.02),
        "conv_ln_g": 1.0 + nrm(ks[6], (N_EVEN, CONV_CH), 0.02),
        "conv_ln_b": nrm(ks[7], (N_EVEN, CONV_CH), 0.02),
        "fox_f_bias": jax.random.uniform(ks[8], (N_EVEN, FOX_HEADS), jnp.float32, 1.0, 4.0),
        "w_out_even": nrm(ks[9], (N_EVEN, MIX_WIDTH, D_MODEL), MIX_WIDTH ** -0.5 * DEEPNORM_BETA),
        "w_in_odd": nrm(ks[10], (N_ODD, D_MODEL, ODD_IN), D_MODEL ** -0.5),
        "ret_norm_g": 1.0 + nrm(ks[11], (N_ODD, GROUP_WIDTH), 0.02),
        "w_out_odd": nrm(ks[12], (N_ODD, MIX_WIDTH, D_MODEL), MIX_WIDTH ** -0.5 * DEEPNORM_BETA),
        "ln_mix_g": 1.0 + nrm(ks[13], (DEPTH, D_MODEL), 0.02),
        "ln_mix_b": nrm(ks[14], (DEPTH, D_MODEL), 0.02),
        "ln_ffn_g": 1.0 + nrm(ks[15], (DEPTH, D_MODEL), 0.02),
        "ln_ffn_b": nrm(ks[16], (DEPTH, D_MODEL), 0.02),
        "w_router": nrm(ks[17], (D_MODEL, N_EXPERTS), D_MODEL ** -0.5),
        "b_router": nrm(ks[18], (N_EXPERTS,), 0.01),
        "w_gate": nrm(ks[19], (DEPTH, N_EXPERTS, D_MODEL, D_EXPERT), D_MODEL ** -0.5),
        "w_up": nrm(ks[20], (DEPTH, N_EXPERTS, D_MODEL, D_EXPERT), D_MODEL ** -0.5),
        "w_down": nrm(ks[21], (DEPTH, N_EXPERTS, D_EXPERT, D_MODEL), D_EXPERT ** -0.5 * DEEPNORM_BETA),
        "w_ple": nrm(ks[22], (DEPTH, PLE_DIM, D_MODEL), PLE_DIM ** -0.5),
        "w_ple_gate": nrm(ks[23], (DEPTH, D_MODEL, D_MODEL), D_MODEL ** -0.5),
    }


def reference(x, p, positions, w_in_even, conv_w, conv_b, conv_ln_g, conv_ln_b, fox_f_bias,
              w_out_even, w_in_odd, ret_norm_g, w_out_odd, ln_mix_g, ln_mix_b, ln_ffn_g,
              ln_ffn_b, w_router, b_router, w_gate, w_up, w_down, w_ple, w_ple_gate):
    gw = GROUP_WIDTH
    even_split = [2 * CONV_CH, 2 * CONV_CH + gw, 2 * CONV_CH + 2 * gw, 2 * CONV_CH + 3 * gw]
    for i in range(DEPTH):
        j = i // 2
        if i % 2 == 0:
            h = x @ w_in_even[j]
            conv_in, fq, fk, fv, ff = jnp.split(h, even_split, axis=-1)
            conv_out = _conformer_conv(conv_in, conv_w[j], conv_b[j], conv_ln_g[j], conv_ln_b[j])
            fox_out = _forgetting_attention(_heads(fq, FOX_HEADS), _heads(fk, FOX_HEADS),
                                            _heads(fv, FOX_HEADS), ff + fox_f_bias[j])
            mixed = jnp.concatenate([conv_out, fox_out.astype(x.dtype)], axis=-1) @ w_out_even[j]
        else:
            h = x @ w_in_odd[j]
            rq, rk, rv, rg, sq, sk, sv = jnp.split(h, 7, axis=-1)
            ret = _retention(_heads(rq, RET_HEADS), _heads(rk, RET_HEADS), _heads(rv, RET_HEADS), positions)
            ret_out = (_head_groupnorm(ret, ret_norm_g[j]) * jax.nn.silu(rg.astype(jnp.float32))).astype(x.dtype)
            sb_out = _stick_breaking_attention(_heads(sq, SB_HEADS), _heads(sk, SB_HEADS), _heads(sv, SB_HEADS))
            mixed = jnp.concatenate([ret_out, sb_out.astype(x.dtype)], axis=-1) @ w_out_odd[j]
        x = _layernorm(DEEPNORM_ALPHA * x + mixed, ln_mix_g[i], ln_mix_b[i])
        ffn = _moe(x, w_router, b_router, w_gate[i], w_up[i], w_down[i])
        x = _layernorm(DEEPNORM_ALPHA * x + ffn, ln_ffn_g[i], ln_ffn_b[i])
        x = x + jax.nn.sigmoid(x @ w_ple_gate[i]) * (p[i] @ w_ple[i])
    return x
```

```python
import functools

import jax
import jax.numpy as jnp
from jax import lax
from jax.experimental import pallas as pl
from jax.experimental.pallas import tpu as pltpu

F32 = jnp.float32
BF16 = jnp.bfloat16

CHUNK = 64
N_GROUPS = 4
RET_HEADS = 8
SB_HEADS = 16
ROPE_BASE = 10000.0
LN_EPS = 1e-5
NEG = -1e30

LANE = 128
VMEM_LIMIT_BYTES = 56 * 1024 * 1024


def _cparams(n_axes):
    return pltpu.CompilerParams(dimension_semantics=("arbitrary",) * n_axes,
                                vmem_limit_bytes=VMEM_LIMIT_BYTES)


def _sigmoid(x):
    return 1.0 / (1.0 + jnp.exp(-x))


def _dot(a, b):
    return jnp.dot(a, b, preferred_element_type=F32)


def _dot_nt(a, b):
    return lax.dot_general(a, b, (((1,), (1,)), ((), ())), preferred_element_type=F32)


def _mm_kernel(a_ref, b_ref, o_ref):
    o_ref[...] = _dot(a_ref[...], b_ref[...]).astype(o_ref.dtype)


def _tile(n, pref):
    t = min(pref, n)
    while n % t or (t % LANE and t != n):
        t -= LANE if t % LANE == 0 else t % LANE
    return t


def _matmul(a, b, out_dtype, tm=1024, tn=512):
    m, kd = a.shape
    n = b.shape[1]
    tm, tn = _tile(m, tm), _tile(n, tn)
    return pl.pallas_call(
        _mm_kernel,
        grid=(m // tm, n // tn),
        in_specs=[pl.BlockSpec((tm, kd), lambda i, j: (i, 0)),
                  pl.BlockSpec((kd, tn), lambda i, j: (0, j))],
        out_specs=pl.BlockSpec((tm, tn), lambda i, j: (i, j)),
        out_shape=jax.ShapeDtypeStruct((m, n), out_dtype),
        compiler_params=_cparams(2),
        name="proj_in",
    )(a, b)


def _mm2_kernel(a1_ref, a2_ref, b1_ref, b2_ref, o_ref):
    o_ref[...] = (_dot(a1_ref[...], b1_ref[...]) + _dot(a2_ref[...], b2_ref[...])).astype(o_ref.dtype)


def _matmul_cat(a1, a2, b, tm=1024, tn=512):
    m, k1 = a1.shape
    k2 = a2.shape[1]
    n = b.shape[1]
    assert k1 == k2 and b.shape[0] == k1 + k2
    tm, tn = _tile(m, tm), _tile(n, tn)
    return pl.pallas_call(
        _mm2_kernel,
        grid=(m // tm, n // tn),
        in_specs=[pl.BlockSpec((tm, k1), lambda i, j: (i, 0)),
                  pl.BlockSpec((tm, k2), lambda i, j: (i, 0)),
                  pl.BlockSpec((k1, tn), lambda i, j: (0, j)),
                  pl.BlockSpec((k2, tn), lambda i, j: (1, j))],
        out_specs=pl.BlockSpec((tm, tn), lambda i, j: (i, j)),
        out_shape=jax.ShapeDtypeStruct((m, n), F32),
        compiler_params=_cparams(2),
        name="proj_out",
    )(a1, a2, b, b)


def _layernorm_rows(t, g, b):
    mu = jnp.mean(t, axis=-1, keepdims=True)
    d = t - mu
    var = jnp.mean(d * d, axis=-1, keepdims=True)
    return d * lax.rsqrt(var + LN_EPS) * g + b


def _res_ln_kernel(x_ref, y_ref, g_ref, b_ref, of_ref, ob_ref, *, alpha):
    o = _layernorm_rows(alpha * x_ref[...] + y_ref[...], g_ref[...], b_ref[...])
    of_ref[...] = o
    ob_ref[...] = o.astype(BF16)


def _res_ln(x, y, g, b, alpha, tm=256):
    m, d = x.shape
    tm = min(tm, m)
    row = pl.BlockSpec((tm, d), lambda i: (i, 0))
    vec = pl.BlockSpec((1, d), lambda i: (0, 0))
    return pl.pallas_call(
        functools.partial(_res_ln_kernel, alpha=alpha),
        grid=(m // tm,),
        in_specs=[row, row, vec, vec],
        out_specs=[row, row],
        out_shape=[jax.ShapeDtypeStruct((m, d), F32), jax.ShapeDtypeStruct((m, d), BF16)],
        compiler_params=_cparams(1),
        name="res_ln",
    )(x, y, g.reshape(1, d), b.reshape(1, d))


def _conv_kernel(a_ref, g_ref, ap_ref, gp_ref, cw_ref, cb_ref, lg_ref, lb_ref, o_ref, buf_ref, y_ref,
                 *, t_rows, halo, width):
    i = pl.program_id(1)
    ch = a_ref.shape[1]
    buf_ref[pl.ds(halo, t_rows), :] = a_ref[...].astype(F32) * _sigmoid(g_ref[...].astype(F32))
    ap = ap_ref[pl.ds(t_rows - halo, halo), :].astype(F32)
    gp = gp_ref[pl.ds(t_rows - halo, halo), :].astype(F32)
    tail = ap * _sigmoid(gp)
    buf_ref[pl.ds(0, halo), :] = jnp.where(i > 0, tail, 0.0)
    for c0 in range(0, ch, LANE):
        acc = jnp.zeros((t_rows, LANE), F32) + cb_ref[:, c0:c0 + LANE]
        for w in range(width):
            off = halo - (width - 1) + w
            acc = acc + buf_ref[pl.ds(off, t_rows), c0:c0 + LANE] * cw_ref[w:w + 1, c0:c0 + LANE]
        y_ref[:, c0:c0 + LANE] = acc
    y = _layernorm_rows(y_ref[...], lg_ref[...], lb_ref[...])
    o_ref[...] = (y * _sigmoid(y)).astype(o_ref.dtype)


def _conv_module(h, batch, seq, ch, conv_w, conv_b, ln_g, ln_b, t_rows=256, halo=32):
    width = conv_w.shape[0]
    t_rows = min(t_rows, seq)
    assert width - 1 <= halo <= t_rows and seq % t_rows == 0
    nt = seq // t_rows
    cur = lambda col: pl.BlockSpec((t_rows, ch), lambda b, i: (b * nt + i, col))
    prev = lambda col: pl.BlockSpec((t_rows, ch), lambda b, i: (b * nt + jnp.maximum(i - 1, 0), col))
    vec = pl.BlockSpec((1, ch), lambda b, i: (0, 0))
    return pl.pallas_call(
        functools.partial(_conv_kernel, t_rows=t_rows, halo=halo, width=width),
        grid=(batch, nt),
        in_specs=[cur(0), cur(1), prev(0), prev(1),
                  pl.BlockSpec((width, ch), lambda b, i: (0, 0)), vec, vec, vec],
        out_specs=pl.BlockSpec((t_rows, ch), lambda b, i: (b * nt + i, 0)),
        out_shape=jax.ShapeDtypeStruct((batch * seq, ch), BF16),
        scratch_shapes=[pltpu.VMEM((t_rows + halo, ch), F32), pltpu.VMEM((t_rows, ch), F32)],
        compiler_params=_cparams(2),
        name="conv_module",
    )(h, h, h, h, conv_w, conv_b.reshape(1, ch), ln_g.reshape(1, ch), ln_b.reshape(1, ch))


def _cum_kernel(f_ref, b_ref, o_ref):
    x = f_ref[...] + b_ref[...]
    ls = jnp.minimum(x, 0.0) - jnp.log(1.0 + jnp.exp(-jnp.abs(x)))
    lane = lax.broadcasted_iota(jnp.int32, ls.shape, 1)
    sh = 1
    while sh < ls.shape[1]:
        ls = ls + jnp.where(lane >= sh, pltpu.roll(ls, sh, 1), 0.0)
        sh *= 2
    o_ref[...] = ls


def _fox_cum(f_rows, bias_rows):
    r, s = f_rows.shape
    return pl.pallas_call(
        _cum_kernel,
        grid=(1,),
        in_specs=[pl.BlockSpec((r, s), lambda i: (0, 0)), pl.BlockSpec((r, 1), lambda i: (0, 0))],
        out_specs=pl.BlockSpec((r, s), lambda i: (0, 0)),
        out_shape=jax.ShapeDtypeStruct((r, s), F32),
        compiler_params=_cparams(1),
        name="fox_cum",
    )(f_rows, bias_rows)


def _fox_kernel(q_ref, k_ref, v_ref, c_ref, o_ref, *, blk):
    i = pl.program_id(2)
    hd = q_ref.shape[1]
    q = q_ref[...]
    row = lax.broadcasted_iota(jnp.int32, (blk, blk), 0)
    col = lax.broadcasted_iota(jnp.int32, (blk, blk), 1)

    def step(kj, carry, masked):
        m, l, acc = carry
        start = pl.multiple_of(kj * blk, blk)
        k = k_ref[pl.ds(start, blk), :]
        v = v_ref[pl.ds(start, blk), :]
        s = _dot_nt(q, k) - c_ref[0, kj]
        if masked:
            s = jnp.where(col <= row, s, NEG)
        m_new = jnp.maximum(m, jnp.max(s, axis=-1, keepdims=True))
        p = jnp.exp(s - m_new)
        alpha = jnp.exp(m - m_new)
        l = alpha * l + jnp.sum(p, axis=-1, keepdims=True)
        acc = alpha * acc + _dot(p.astype(BF16), v)
        return m_new, l, acc

    init = (jnp.full((blk, 1), NEG, F32), jnp.zeros((blk, 1), F32), jnp.zeros((blk, hd), F32))
    carry = lax.fori_loop(0, i, lambda kj, c: step(kj, c, False), init)
    _, l, acc = step(i, carry, True)
    o_ref[...] = (acc / l).astype(o_ref.dtype)


def _fox_attention(h, cum, batch, seq, n_heads, hd, q_col, k_col, v_col, blk=512):
    blk = min(blk, seq)
    nq = seq // blk
    return pl.pallas_call(
        functools.partial(_fox_kernel, blk=blk),
        grid=(batch, n_heads, nq),
        in_specs=[pl.BlockSpec((blk, hd), lambda b, hh, i: (b * nq + i, q_col + hh)),
                  pl.BlockSpec((seq, hd), lambda b, hh, i: (b, k_col + hh)),
                  pl.BlockSpec((seq, hd), lambda b, hh, i: (b, v_col + hh)),
                  pl.BlockSpec((1, nq, 1, blk), lambda b, hh, i: (b * n_heads + hh, 0, 0, 0))],
        out_specs=pl.BlockSpec((blk, hd), lambda b, hh, i: (b * nq + i, hh)),
        out_shape=jax.ShapeDtypeStruct((batch * seq, n_heads * hd), BF16),
        compiler_params=_cparams(3),
        name="fox_attention",
    )(h, h, h, cum)


def _sb_kernel(q_ref, k_ref, v_ref, o_ref, *, tq, tk):
    i = pl.program_id(2)
    hd = q_ref.shape[1]
    nsub = tq // tk
    q = q_ref[...]
    jr = lax.broadcasted_iota(jnp.int32, (tk, tk), 0)
    jc = lax.broadcasted_iota(jnp.int32, (tk, tk), 1)
    upper = jnp.where(jr > jc, 1.0, 0.0).astype(BF16)
    row = i * tq + lax.broadcasted_iota(jnp.int32, (tq, tk), 0)
    col = lax.broadcasted_iota(jnp.int32, (tq, tk), 1)

    def step(kj, carry, masked):
        run, acc = carry
        start = pl.multiple_of(kj * tk, tk)
        k = k_ref[pl.ds(start, tk), :]
        v = v_ref[pl.ds(start, tk), :]
        z = _dot_nt(q, k)
        sp = jnp.maximum(z, 0.0) + jnp.log(1.0 + jnp.exp(-jnp.abs(z)))
        log_1m = -sp
        log_b = z - sp
        if masked:
            valid = (start + col) < row
            log_1m = jnp.where(valid, log_1m, 0.0)
        hi = log_1m.astype(BF16)
        lo = (log_1m - hi.astype(F32)).astype(BF16)
        later = _dot(hi, upper) + _dot(lo, upper)
        w = jnp.exp(log_b + later + run)
        if masked:
            w = jnp.where(valid, w, 0.0)
        acc = acc + _dot(w.astype(BF16), v)
        run = run + later[:, 0:1] + log_1m[:, 0:1]
        return run, acc

    carry = (jnp.zeros((tq, 1), F32), jnp.zeros((tq, hd), F32))
    for sub in range(nsub - 1, -1, -1):
        carry = step(i * nsub + sub, carry, True)
    nprev = i * nsub
    _, acc = lax.fori_loop(0, nprev, lambda t, c: step(nprev - 1 - t, c, False), carry)
    o_ref[...] = acc.astype(o_ref.dtype)


def _sb_attention(h, batch, seq, n_heads, hd, q_col, k_col, v_col, tq=256, tk=128):
    tq, tk = min(tq, seq), min(tk, seq)
    nq = seq // tq
    return pl.pallas_call(
        functools.partial(_sb_kernel, tq=tq, tk=tk),
        grid=(batch, n_heads, nq),
        in_specs=[pl.BlockSpec((tq, hd), lambda b, hh, i: (b * nq + i, q_col + hh)),
                  pl.BlockSpec((seq, hd), lambda b, hh, i: (b, k_col + hh)),
                  pl.BlockSpec((seq, hd), lambda b, hh, i: (b, v_col + hh))],
        out_specs=pl.BlockSpec((tq, hd), lambda b, hh, i: (b * nq + i, hh)),
        out_shape=jax.ShapeDtypeStruct((batch * seq, n_heads * hd), BF16),
        compiler_params=_cparams(3),
        name="sb_attention",
    )(h, h, h)


def _rope_kernel(pos_ref, inv_ref, cos_ref, sin_ref):
    ang = pos_ref[...] * inv_ref[...]
    cos_ref[...] = jnp.cos(ang)
    sin_ref[...] = jnp.sin(ang)


def _rope_tables(pos_col, inv_row, tm=512):
    m = pos_col.shape[0]
    half = inv_row.shape[1]
    tm = min(tm, m)
    out = pl.BlockSpec((tm, half), lambda i: (i, 0))
    return pl.pallas_call(
        _rope_kernel,
        grid=(m // tm,),
        in_specs=[pl.BlockSpec((tm, 1), lambda i: (i, 0)), pl.BlockSpec((1, half), lambda i: (0, 0))],
        out_specs=[out, out],
        out_shape=[jax.ShapeDtypeStruct((m, half), F32)] * 2,
        compiler_params=_cparams(1),
        name="rope_tables",
    )(pos_col, inv_row)


def _ret_kernel(lg_ref, q_ref, k_ref, v_ref, gate_ref, cos_ref, sin_ref, gn_ref, o_ref, state_ref, *, blk):
    hh = pl.program_id(1)
    i = pl.program_id(2)
    d = q_ref.shape[1]
    half = d // 2
    lg = lg_ref[hh]

    @pl.when(i == 0)
    def _():
        state_ref[...] = jnp.zeros_like(state_ref)

    cos = cos_ref[...]
    sin = sin_ref[...]

    def rot(t_ref):
        t = t_ref[...].astype(F32)
        t1, t2 = t[:, :half], t[:, half:]
        return jnp.concatenate([t1 * cos - t2 * sin, t1 * sin + t2 * cos], axis=1)

    qr = rot(q_ref)
    kr = rot(k_ref)
    v = v_ref[...]
    n = lax.broadcasted_iota(jnp.int32, (blk, 1), 0).astype(F32)
    q_decay = jnp.exp((n + 1.0) * lg)
    k_decay = jnp.exp((blk - 1.0 - n) * lg)
    row = lax.broadcasted_iota(jnp.int32, (blk, blk), 0)
    col = lax.broadcasted_iota(jnp.int32, (blk, blk), 1)
    dist = jnp.abs(row - col).astype(F32)
    shift = CHUNK.bit_length() - 1
    seen = (col >> shift) <= (row >> shift)
    decay = jnp.where(seen, jnp.exp(dist * lg), 0.0)
    scores = _dot_nt(qr.astype(BF16), kr.astype(BF16)) * decay
    out = _dot(scores.astype(BF16), v) + _dot((qr * q_decay).astype(BF16), state_ref[...].astype(BF16))
    kv = lax.dot_general((kr * k_decay).astype(BF16), v, (((0,), (0,)), ((), ())), preferred_element_type=F32)
    state_ref[...] = jnp.exp(jnp.full((1, d), blk * lg, F32)) * state_ref[...] + kv
    mu = jnp.mean(out, axis=-1, keepdims=True)
    dv = out - mu
    var = jnp.mean(dv * dv, axis=-1, keepdims=True)
    y = dv * lax.rsqrt(var + LN_EPS) * gn_ref[...]
    g = gate_ref[...].astype(F32)
    o_ref[...] = (y * (g * _sigmoid(g))).astype(o_ref.dtype)


def _retention(h, cos, sin, log_g, gn, batch, seq, n_heads, d, blk=256):
    blk = min(blk, seq)
    assert blk % CHUNK == 0 and CHUNK & (CHUNK - 1) == 0
    nb = seq // blk
    hcol = lambda off: pl.BlockSpec((blk, d), lambda b, hh, i: (b * nb + i, off + hh))
    tab = pl.BlockSpec((blk, d // 2), lambda b, hh, i: (b * nb + i, 0))
    return pl.pallas_call(
        functools.partial(_ret_kernel, blk=blk),
        grid=(batch, n_heads, nb),
        in_specs=[pl.BlockSpec(memory_space=pltpu.SMEM),
                  hcol(0), hcol(n_heads), hcol(2 * n_heads), hcol(3 * n_heads), tab, tab,
                  pl.BlockSpec((1, d), lambda b, hh, i: (0, hh))],
        out_specs=pl.BlockSpec((blk, d), lambda b, hh, i: (b * nb + i, hh)),
        out_shape=jax.ShapeDtypeStruct((batch * seq, n_heads * d), BF16),
        scratch_shapes=[pltpu.VMEM((d, d), F32)],
        compiler_params=_cparams(3),
        name="retention",
    )(log_g, h, h, h, h, cos, sin, gn.reshape(1, n_heads * d))


def _router_kernel(x_ref, w_ref, b_ref, o_ref, *, n_exp, epg):
    logits = _dot(x_ref[...], w_ref[...]) + b_ref[...]
    lane = lax.broadcasted_iota(jnp.int32, logits.shape, 1)
    lmax = jnp.max(logits, axis=-1, keepdims=True)
    p = jnp.where(lane < n_exp, jnp.exp(logits - lmax), -1.0)
    p1 = jnp.max(p, axis=-1, keepdims=True)
    e1 = jnp.min(jnp.where(p == p1, lane, LANE), axis=-1, keepdims=True)
    shift = epg.bit_length() - 1
    in_group = (lane >> shift) == (e1 >> shift)
    cand = jnp.where(in_group, jnp.where(lane == e1, -1.0, p), -1.0)
    p2 = jnp.max(cand, axis=-1, keepdims=True)
    e2 = jnp.min(jnp.where(cand == p2, lane, LANE), axis=-1, keepdims=True)
    tot = p1 + p2
    o_ref[...] = jnp.where(lane == 0, e1.astype(F32),
                           jnp.where(lane == 1, e2.astype(F32),
                                     jnp.where(lane == 2, p1 / tot, jnp.where(lane == 3, p2 / tot, 0.0))))


def _router(xb, w_pad, b_pad, n_exp, tm=512):
    m, d = xb.shape
    tm = min(tm, m)
    epg = n_exp // N_GROUPS
    assert epg & (epg - 1) == 0 and n_exp <= LANE
    return pl.pallas_call(
        functools.partial(_router_kernel, n_exp=n_exp, epg=epg),
        grid=(m // tm,),
        in_specs=[pl.BlockSpec((tm, d), lambda i: (i, 0)),
                  pl.BlockSpec((d, LANE), lambda i: (0, 0)),
                  pl.BlockSpec((1, LANE), lambda i: (0, 0))],
        out_specs=pl.BlockSpec((tm, LANE), lambda i: (i, 0)),
        out_shape=jax.ShapeDtypeStruct((m, LANE), F32),
        compiler_params=_cparams(1),
        name="router",
    )(xb, w_pad, b_pad)


def _gather_rows(idx_ref, n_rows, src_hbm, dst_ref, sem):
    def issue(r, carry):
        pltpu.make_async_copy(src_hbm.at[pl.ds(idx_ref[0, 0, r], 1), :], dst_ref.at[pl.ds(r, 1), :], sem).start()
        return carry
    lax.fori_loop(0, n_rows, issue, 0)

    def drain(r, carry):
        pltpu.make_async_copy(src_hbm.at[pl.ds(0, 1), :], dst_ref.at[pl.ds(r, 1), :], sem).wait()
        return carry
    lax.fori_loop(0, n_rows, drain, 0)


def _moe_kernel(te_ref, tv_ref, src_ref, x_hbm, wg_ref, wu_ref, wd_ref, y_ref, xbuf, sem, *, t_rows):
    t = pl.program_id(0)

    @pl.when(tv_ref[t] == 1)
    def _():
        _gather_rows(src_ref, t_rows, x_hbm, xbuf, sem.at[0])
        xb = xbuf[...].astype(BF16)
        hg = _dot(xb, wg_ref[0, 0])
        hu = _dot(xb, wu_ref[0, 0])
        hidden = (hg * _sigmoid(hg) * hu).astype(BF16)
        y_ref[...] = _dot(hidden, wd_ref[0, 0])

    @pl.when(tv_ref[t] == 0)
    def _():
        y_ref[...] = jnp.zeros_like(y_ref)


def _moe_ffn(x, src_rows, tile_expert, tile_valid, w_gate, w_up, w_down, layer, t_rows):
    n_tiles = tile_expert.shape[0]
    d = x.shape[1]
    f = w_gate.shape[3]
    wspec = lambda a, b: pl.BlockSpec((1, 1, a, b), lambda t, te, tv: (layer, te[t], 0, 0))
    grid_spec = pltpu.PrefetchScalarGridSpec(
        num_scalar_prefetch=2,
        grid=(n_tiles,),
        in_specs=[pl.BlockSpec((1, 1, t_rows), lambda t, te, tv: (t, 0, 0), memory_space=pltpu.SMEM),
                  pl.BlockSpec(memory_space=pl.ANY),
                  wspec(d, f), wspec(d, f), wspec(f, d)],
        out_specs=pl.BlockSpec((t_rows, d), lambda t, te, tv: (t, 0)),
        scratch_shapes=[pltpu.VMEM((t_rows, d), F32), pltpu.SemaphoreType.DMA((1,))],
    )
    return pl.pallas_call(
        functools.partial(_moe_kernel, t_rows=t_rows),
        grid_spec=grid_spec,
        out_shape=jax.ShapeDtypeStruct((n_tiles * t_rows, d), F32),
        compiler_params=_cparams(1),
        name="moe_ffn",
    )(tile_expert, tile_valid, src_rows.reshape(n_tiles, 1, t_rows), x, w_gate, w_up, w_down)


def _combine_kernel(slot_ref, route_ref, x_ref, y_hbm, g_ref, b_ref, of_ref, ob_ref, buf, sem, *, tm, alpha):
    _gather_rows(slot_ref, 2 * tm, y_hbm, buf, sem.at[0])
    w1 = route_ref[:, 2:3]
    w2 = route_ref[:, 3:4]
    ffn = w1 * buf[pl.ds(0, tm), :] + w2 * buf[pl.ds(tm, tm), :]
    o = _layernorm_rows(alpha * x_ref[...] + ffn, g_ref[...], b_ref[...])
    of_ref[...] = o
    ob_ref[...] = o.astype(BF16)


def _moe_combine_ln(x, y, route, slots, g, b, alpha, tm=128):
    m, d = x.shape
    row = pl.BlockSpec((tm, d), lambda i: (i, 0))
    vec = pl.BlockSpec((1, d), lambda i: (0, 0))
    return pl.pallas_call(
        functools.partial(_combine_kernel, tm=tm, alpha=alpha),
        grid=(m // tm,),
        in_specs=[pl.BlockSpec((1, 1, 2 * tm), lambda i: (i, 0, 0), memory_space=pltpu.SMEM),
                  pl.BlockSpec((tm, LANE), lambda i: (i, 0)),
                  row,
                  pl.BlockSpec(memory_space=pl.ANY),
                  vec, vec],
        out_specs=[row, row],
        out_shape=[jax.ShapeDtypeStruct((m, d), F32), jax.ShapeDtypeStruct((m, d), BF16)],
        scratch_shapes=[pltpu.VMEM((2 * tm, d), F32), pltpu.SemaphoreType.DMA((1,))],
        compiler_params=_cparams(1),
        name="moe_combine_ln",
    )(slots, route, x, y, g.reshape(1, d), b.reshape(1, d))


def _routing_tables(route, n_exp, t_rows, tm):
    m = route.shape[0]
    e = route[:, 0:2].astype(jnp.int32)
    flat_e = e.reshape(-1)
    onehot = (flat_e[:, None] == jnp.arange(n_exp, dtype=jnp.int32)[None, :]).astype(jnp.int32)
    csum = jnp.cumsum(onehot, axis=0)
    counts = csum[-1]
    rank = jnp.take_along_axis(csum, flat_e[:, None], axis=1)[:, 0] - 1
    padded = ((counts + t_rows - 1) // t_rows) * t_rows
    ends = jnp.cumsum(padded)
    starts = ends - padded
    slot = starts[flat_e] + rank
    n_slots = 2 * m + n_exp * t_rows
    n_tiles = n_slots // t_rows
    src_rows = jnp.zeros((n_slots,), jnp.int32).at[slot].set(jnp.arange(2 * m, dtype=jnp.int32) // 2)
    tile_start = jnp.arange(n_tiles, dtype=jnp.int32) * t_rows
    tile_valid = (tile_start < ends[-1]).astype(jnp.int32)
    last_valid = jnp.maximum(ends[-1] - 1, 0)
    tile_expert = jnp.searchsorted(ends, jnp.minimum(tile_start, last_valid), side="right").astype(jnp.int32)
    tile_expert = jnp.minimum(tile_expert, n_exp - 1)
    slot2 = slot.reshape(m, 2)
    slots = jnp.concatenate([slot2[:, 0].reshape(m // tm, 1, tm), slot2[:, 1].reshape(m // tm, 1, tm)], axis=2)
    return src_rows, tile_expert, tile_valid, slots


def _ple_kernel(a_ref, w_ref, x_ref, p_ref, wp_ref, of_ref, ob_ref):
    gate = _sigmoid(_dot(a_ref[...], w_ref[0]))
    o = x_ref[...] + gate * _dot(p_ref[0], wp_ref[0])
    of_ref[...] = o
    ob_ref[...] = o.astype(BF16)


def _ple(xb, xf, p_all, w_gate_all, w_ple_all, layer, tm=1024, tn=512):
    m, d = xb.shape
    pd = p_all.shape[2]
    tm, tn = _tile(m, tm), _tile(d, tn)
    out = pl.BlockSpec((tm, tn), lambda i, j: (i, j))
    return pl.pallas_call(
        _ple_kernel,
        grid=(m // tm, d // tn),
        in_specs=[pl.BlockSpec((tm, d), lambda i, j: (i, 0)),
                  pl.BlockSpec((1, d, tn), lambda i, j: (layer, 0, j)),
                  out,
                  pl.BlockSpec((1, tm, pd), lambda i, j: (layer, i, 0)),
                  pl.BlockSpec((1, pd, tn), lambda i, j: (layer, 0, j))],
        out_specs=[out, out],
        out_shape=[jax.ShapeDtypeStruct((m, d), F32), jax.ShapeDtypeStruct((m, d), BF16)],
        compiler_params=_cparams(2),
        name="ple",
    )(xb, w_gate_all, xf, p_all, w_ple_all)


def _deinterleave_heads(w, n_heads):
    kd, width = w.shape
    d = width // n_heads
    w4 = w.reshape(kd, n_heads, d // 2, 2)
    return jnp.concatenate([w4[..., 0], w4[..., 1]], axis=-1).reshape(kd, width)


def kernel(x, p, positions, w_in_even, conv_w, conv_b, conv_ln_g, conv_ln_b, fox_f_bias, w_out_even, w_in_odd, ret_norm_g, w_out_odd, ln_mix_g, ln_mix_b, ln_ffn_g, ln_ffn_b, w_router, b_router, w_gate, w_up, w_down, w_ple, w_ple_gate):
    batch, seq, d_model = x.shape
    depth = ln_mix_g.shape[0]
    m = batch * seq
    gw = d_model // 2
    ch = conv_w.shape[2]
    fox_heads = fox_f_bias.shape[1]
    fox_hd = gw // fox_heads
    ret_hd = gw // RET_HEADS
    sb_hd = gw // SB_HEADS
    n_exp = w_router.shape[1]
    alpha = (2 * depth) ** 0.25
    moe_rows = min(256, m)
    comb_rows = min(128, m)
    fox_blk = min(512, seq)

    xf = x.reshape(m, d_model)
    xb = xf.astype(BF16)
    p_b = p.reshape(depth, m, p.shape[-1]).astype(BF16)
    w_ple_b = w_ple.astype(BF16)
    w_ple_gate_b = w_ple_gate.astype(BF16)
    w_gate_b, w_up_b, w_down_b = w_gate.astype(BF16), w_up.astype(BF16), w_down.astype(BF16)
    w_router_pad = jnp.zeros((d_model, LANE), F32).at[:, :n_exp].set(w_router).astype(BF16)
    b_router_pad = jnp.full((1, LANE), NEG, F32).at[0, :n_exp].set(b_router)

    inv = ROPE_BASE ** (-jnp.arange(0, ret_hd, 2, dtype=F32) / ret_hd)
    log_g = jnp.log1p(-jnp.exp2(-5.0 - jnp.arange(RET_HEADS, dtype=F32)))
    cos, sin = _rope_tables(positions.reshape(m, 1).astype(F32), inv.reshape(1, ret_hd // 2))

    for i in range(depth):
        j = i // 2
        if i % 2 == 0:
            w_in = w_in_even[j]
            n_main = 2 * ch + 3 * gw
            scale_cols = jnp.ones((n_main,), F32).at[2 * ch:2 * ch + gw].set(fox_hd ** -0.5)
            h = _matmul(xb, (w_in[:, :n_main] * scale_cols).astype(BF16), BF16)
            w_f = jnp.zeros((d_model, LANE), F32).at[:, :fox_heads].set(w_in[:, n_main:]).astype(BF16)
            f_logit = _matmul(xb, w_f, F32)[:, :fox_heads]
            f_rows = f_logit.reshape(batch, seq, fox_heads).transpose(0, 2, 1).reshape(batch * fox_heads, seq)
            bias_rows = jnp.tile(fox_f_bias[j], batch).reshape(batch * fox_heads, 1)
            cum = _fox_cum(f_rows, bias_rows).reshape(batch * fox_heads, seq // fox_blk, 1, fox_blk)
            conv_out = _conv_module(h, batch, seq, ch, conv_w[j], conv_b[j], conv_ln_g[j], conv_ln_b[j])
            q_col = 2 * ch // fox_hd
            fox_out = _fox_attention(h, cum, batch, seq, fox_heads, fox_hd,
                                     q_col, q_col + fox_heads, q_col + 2 * fox_heads, blk=fox_blk)
            mixed = _matmul_cat(conv_out, fox_out, w_out_even[j].astype(BF16))
        else:
            w_in = w_in_odd[j]
            w_rq = _deinterleave_heads(w_in[:, :gw], RET_HEADS) * (ret_hd ** -0.5)
            w_rk = _deinterleave_heads(w_in[:, gw:2 * gw], RET_HEADS)
            w_sq = w_in[:, 4 * gw:5 * gw] * (sb_hd ** -0.5)
            w_odd = jnp.concatenate([w_rq, w_rk, w_in[:, 2 * gw:4 * gw], w_sq, w_in[:, 5 * gw:]], axis=1)
            h = _matmul(xb, w_odd.astype(BF16), BF16)
            ret_out = _retention(h, cos, sin, log_g, ret_norm_g[j], batch, seq, RET_HEADS, ret_hd)
            s_col = 4 * gw // sb_hd
            sb_out = _sb_attention(h, batch, seq, SB_HEADS, sb_hd, s_col, s_col + SB_HEADS, s_col + 2 * SB_HEADS)
            mixed = _matmul_cat(ret_out, sb_out, w_out_odd[j].astype(BF16))
        xf, xb = _res_ln(xf, mixed, ln_mix_g[i], ln_mix_b[i], alpha)
        route = _router(xb, w_router_pad, b_router_pad, n_exp)
        src_rows, tile_expert, tile_valid, slots = _routing_tables(route, n_exp, moe_rows, comb_rows)
        y = _moe_ffn(xf, src_rows, tile_expert, tile_valid, w_gate_b, w_up_b, w_down_b, i, moe_rows)
        xf, xb = _moe_combine_ln(xf, y, route, slots, ln_ffn_g[i], ln_ffn_b[i], alpha, tm=comb_rows)
        xf, xb = _ple(xb, xf, p_b, w_ple_gate_b, w_ple_b, i)
    return xf.reshape(batch, seq, d_model)
```

```python
import functools

import jax
import jax.numpy as jnp
from jax import lax
from jax.experimental import pallas as pl
from jax.experimental.pallas import tpu as pltpu

F32 = jnp.float32
BF16 = jnp.bfloat16

CHUNK = 64
N_GROUPS = 4
RET_HEADS = 8
SB_HEADS = 16
ROPE_BASE = 10000.0
LN_EPS = 1e-5
NEG = -1e30

LANE = 128
VMEM_LIMIT_BYTES = 56 * 1024 * 1024


def _cparams(n_axes):
    return pltpu.CompilerParams(dimension_semantics=("arbitrary",) * n_axes,
                                vmem_limit_bytes=VMEM_LIMIT_BYTES)


def _sigmoid(x):
    return 1.0 / (1.0 + jnp.exp(-x))


def _dot(a, b):
    return jnp.dot(a, b, preferred_element_type=F32)


def _dot_nt(a, b):
    return lax.dot_general(a, b, (((1,), (1,)), ((), ())), preferred_element_type=F32)


def _mm_kernel(a_ref, b_ref, o_ref):
    o_ref[...] = _dot(a_ref[...], b_ref[...]).astype(o_ref.dtype)


def _tile(n, pref):
    t = min(pref, n)
    while n % t or (t % LANE and t != n):
        t -= LANE if t % LANE == 0 else t % LANE
    return t


def _matmul(a, b, out_dtype, tm=1024, tn=512):
    m, kd = a.shape
    n = b.shape[1]
    tm, tn = _tile(m, tm), _tile(n, tn)
    return pl.pallas_call(
        _mm_kernel,
        grid=(m // tm, n // tn),
        in_specs=[pl.BlockSpec((tm, kd), lambda i, j: (i, 0)),
                  pl.BlockSpec((kd, tn), lambda i, j: (0, j))],
        out_specs=pl.BlockSpec((tm, tn), lambda i, j: (i, j)),
        out_shape=jax.ShapeDtypeStruct((m, n), out_dtype),
        compiler_params=_cparams(2),
        name="proj_in",
    )(a, b)


def _mm2_kernel(a1_ref, a2_ref, b1_ref, b2_ref, o_ref):
    o_ref[...] = (_dot(a1_ref[...], b1_ref[...]) + _dot(a2_ref[...], b2_ref[...])).astype(o_ref.dtype)


def _matmul_cat(a1, a2, b, tm=1024, tn=512):
    m, k1 = a1.shape
    k2 = a2.shape[1]
    n = b.shape[1]
    assert k1 == k2 and b.shape[0] == k1 + k2
    tm, tn = _tile(m, tm), _tile(n, tn)
    return pl.pallas_call(
        _mm2_kernel,
        grid=(m // tm, n // tn),
        in_specs=[pl.BlockSpec((tm, k1), lambda i, j: (i, 0)),
                  pl.BlockSpec((tm, k2), lambda i, j: (i, 0)),
                  pl.BlockSpec((k1, tn), lambda i, j: (0, j)),
                  pl.BlockSpec((k2, tn), lambda i, j: (1, j))],
        out_specs=pl.BlockSpec((tm, tn), lambda i, j: (i, j)),
        out_shape=jax.ShapeDtypeStruct((m, n), F32),
        compiler_params=_cparams(2),
        name="proj_out",
    )(a1, a2, b, b)


def _layernorm_rows(t, g, b):
    mu = jnp.mean(t, axis=-1, keepdims=True)
    d = t - mu
    var = jnp.mean(d * d, axis=-1, keepdims=True)
    return d * lax.rsqrt(var + LN_EPS) * g + b


def _res_ln_kernel(x_ref, y_ref, g_ref, b_ref, of_ref, ob_ref, *, alpha):
    o = _layernorm_rows(alpha * x_ref[...] + y_ref[...], g_ref[...], b_ref[...])
    of_ref[...] = o
    ob_ref[...] = o.astype(BF16)


def _res_ln(x, y, g, b, alpha, tm=256):
    m, d = x.shape
    tm = min(tm, m)
    row = pl.BlockSpec((tm, d), lambda i: (i, 0))
    vec = pl.BlockSpec((1, d), lambda i: (0, 0))
    return pl.pallas_call(
        functools.partial(_res_ln_kernel, alpha=alpha),
        grid=(m // tm,),
        in_specs=[row, row, vec, vec],
        out_specs=[row, row],
        out_shape=[jax.ShapeDtypeStruct((m, d), F32), jax.ShapeDtypeStruct((m, d), BF16)],
        compiler_params=_cparams(1),
        name="res_ln",
    )(x, y, g.reshape(1, d), b.reshape(1, d))


def _conv_kernel(a_ref, g_ref, ap_ref, gp_ref, cw_ref, cb_ref, lg_ref, lb_ref, o_ref, buf_ref, y_ref,
                 *, t_rows, halo, width):
    i = pl.program_id(1)
    ch = a_ref.shape[1]
    buf_ref[pl.ds(halo, t_rows), :] = a_ref[...].astype(F32) * _sigmoid(g_ref[...].astype(F32))
    ap = ap_ref[pl.ds(t_rows - halo, halo), :].astype(F32)
    gp = gp_ref[pl.ds(t_rows - halo, halo), :].astype(F32)
    tail = ap * _sigmoid(gp)
    buf_ref[pl.ds(0, halo), :] = jnp.where(i > 0, tail, 0.0)
    for c0 in range(0, ch, LANE):
        acc = jnp.zeros((t_rows, LANE), F32) + cb_ref[:, c0:c0 + LANE]
        for w in range(width):
            off = halo - (width - 1) + w
            acc = acc + buf_ref[pl.ds(off, t_rows), c0:c0 + LANE] * cw_ref[w:w + 1, c0:c0 + LANE]
        y_ref[:, c0:c0 + LANE] = acc
    y = _layernorm_rows(y_ref[...], lg_ref[...], lb_ref[...])
    o_ref[...] = (y * _sigmoid(y)).astype(o_ref.dtype)


def _conv_module(h, batch, seq, ch, conv_w, conv_b, ln_g, ln_b, t_rows=256, halo=32):
    width = conv_w.shape[0]
    t_rows = min(t_rows, seq)
    assert width - 1 <= halo <= t_rows and seq % t_rows == 0
    nt = seq // t_rows
    cur = lambda col: pl.BlockSpec((t_rows, ch), lambda b, i: (b * nt + i, col))
    prev = lambda col: pl.BlockSpec((t_rows, ch), lambda b, i: (b * nt + jnp.maximum(i - 1, 0), col))
    vec = pl.BlockSpec((1, ch), lambda b, i: (0, 0))
    return pl.pallas_call(
        functools.partial(_conv_kernel, t_rows=t_rows, halo=halo, width=width),
        grid=(batch, nt),
        in_specs=[cur(0), cur(1), prev(0), prev(1),
                  pl.BlockSpec((width, ch), lambda b, i: (0, 0)), vec, vec, vec],
        out_specs=pl.BlockSpec((t_rows, ch), lambda b, i: (b * nt + i, 0)),
        out_shape=jax.ShapeDtypeStruct((batch * seq, ch), BF16),
        scratch_shapes=[pltpu.VMEM((t_rows + halo, ch), F32), pltpu.VMEM((t_rows, ch), F32)],
        compiler_params=_cparams(2),
        name="conv_module",
    )(h, h, h, h, conv_w, conv_b.reshape(1, ch), ln_g.reshape(1, ch), ln_b.reshape(1, ch))


def _cum_kernel(f_ref, b_ref, o_ref):
    x = f_ref[...] + b_ref[...]
    ls = jnp.minimum(x, 0.0) - jnp.log(1.0 + jnp.exp(-jnp.abs(x)))
    lane = lax.broadcasted_iota(jnp.int32, ls.shape, 1)
    sh = 1
    while sh < ls.shape[1]:
        ls = ls + jnp.where(lane >= sh, pltpu.roll(ls, sh, 1), 0.0)
        sh *= 2
    o_ref[...] = ls


def _fox_cum(f_rows, bias_rows):
    r, s = f_rows.shape
    return pl.pallas_call(
        _cum_kernel,
        grid=(1,),
        in_specs=[pl.BlockSpec((r, s), lambda i: (0, 0)), pl.BlockSpec((r, 1), lambda i: (0, 0))],
        out_specs=pl.BlockSpec((r, s), lambda i: (0, 0)),
        out_shape=jax.ShapeDtypeStruct((r, s), F32),
        compiler_params=_cparams(1),
        name="fox_cum",
    )(f_rows, bias_rows)


def _fox_kernel(q_ref, k_ref, v_ref, c_ref, o_ref, *, blk):
    i = pl.program_id(2)
    hd = q_ref.shape[1]
    q = q_ref[...]
    row = lax.broadcasted_iota(jnp.int32, (blk, blk), 0)
    col = lax.broadcasted_iota(jnp.int32, (blk, blk), 1)

    def step(kj, carry, masked):
        m, l, acc = carry
        start = pl.multiple_of(kj * blk, blk)
        k = k_ref[pl.ds(start, blk), :]
        v = v_ref[pl.ds(start, blk), :]
        s = _dot_nt(q, k) - c_ref[0, kj]
        if masked:
            s = jnp.where(col <= row, s, NEG)
        m_new = jnp.maximum(m, jnp.max(s, axis=-1, keepdims=True))
        p = jnp.exp(s - m_new)
        alpha = jnp.exp(m - m_new)
        l = alpha * l + jnp.sum(p, axis=-1, keepdims=True)
        acc = alpha * acc + _dot(p.astype(BF16), v)
        return m_new, l, acc

    init = (jnp.full((blk, 1), NEG, F32), jnp.zeros((blk, 1), F32), jnp.zeros((blk, hd), F32))
    carry = lax.fori_loop(0, i, lambda kj, c: step(kj, c, False), init)
    _, l, acc = step(i, carry, True)
    o_ref[...] = (acc / l).astype(o_ref.dtype)


def _fox_attention(h, cum, batch, seq, n_heads, hd, q_col, k_col, v_col, blk=512):
    blk = min(blk, seq)
    nq = seq // blk
    return pl.pallas_call(
        functools.partial(_fox_kernel, blk=blk),
        grid=(batch, n_heads, nq),
        in_specs=[pl.BlockSpec((blk, hd), lambda b, hh, i: (b * nq + i, q_col + hh)),
                  pl.BlockSpec((seq, hd), lambda b, hh, i: (b, k_col + hh)),
                  pl.BlockSpec((seq, hd), lambda b, hh, i: (b, v_col + hh)),
                  pl.BlockSpec((1, nq, 1, blk), lambda b, hh, i: (b * n_heads + hh, 0, 0, 0))],
        out_specs=pl.BlockSpec((blk, hd), lambda b, hh, i: (b * nq + i, hh)),
        out_shape=jax.ShapeDtypeStruct((batch * seq, n_heads * hd), BF16),
        compiler_params=_cparams(3),
        name="fox_attention",
    )(h, h, h, cum)


SB_DEAD = 105.0


def _sb_kernel(q_ref, k_ref, v_ref, o_ref, *, blk, heads, hd):
    i = pl.program_id(2)
    jr = lax.broadcasted_iota(jnp.int32, (blk, blk), 0)
    jc = lax.broadcasted_iota(jnp.int32, (blk, blk), 1)
    upper = jnp.where(jr > jc, 1.0, 0.0).astype(BF16)
    strict = jc < jr

    def block(hh, start, run, acc, masked):
        cols = slice(hh * hd, (hh + 1) * hd)
        z = _dot_nt(q_ref[:, cols], k_ref[pl.ds(start, blk), cols])
        sp = jnp.maximum(z, 0.0) + jnp.log(1.0 + jnp.exp(-jnp.abs(z)))
        log_1m = -sp
        log_b = z - sp
        if masked:
            log_1m = jnp.where(strict, log_1m, 0.0)
        hi = log_1m.astype(BF16)
        lo = (log_1m - hi.astype(F32)).astype(BF16)
        later = _dot(hi, upper) + _dot(lo, upper)
        w = jnp.exp(log_b + later + run)
        if masked:
            w = jnp.where(strict, w, 0.0)
        acc = acc + _dot(w.astype(BF16), v_ref[pl.ds(start, blk), cols])
        return run + later[:, 0:1] + log_1m[:, 0:1], acc

    def step(kj, carry, masked):
        start = pl.multiple_of(kj * blk, blk)
        return tuple(block(hh, start, *carry[hh], masked) for hh in range(heads))

    def live(carry):
        return functools.reduce(jnp.maximum, [jnp.max(run) for run, _ in carry])

    carry = tuple((jnp.zeros((blk, 1), F32), jnp.zeros((blk, hd), F32)) for _ in range(heads))
    carry = step(i, carry, True)

    def cond(state):
        t, top, _ = state
        return jnp.logical_and(t < i, top > -SB_DEAD)

    def body(state):
        t, _, c = state
        c = step(i - 1 - t, c, False)
        return t + 1, live(c), c

    _, _, carry = lax.while_loop(cond, body, (jnp.int32(0), live(carry), carry))
    for hh in range(heads):
        o_ref[:, hh * hd:(hh + 1) * hd] = carry[hh][1].astype(o_ref.dtype)


def _sb_attention(h, batch, seq, n_heads, hd, q_col, k_col, v_col, blk=256, heads=2):
    blk = min(blk, seq)
    nq = seq // blk
    assert n_heads % heads == 0 and q_col % heads == 0 and k_col % heads == 0 and v_col % heads == 0
    wide = heads * hd
    return pl.pallas_call(
        functools.partial(_sb_kernel, blk=blk, heads=heads, hd=hd),
        grid=(batch, n_heads // heads, nq),
        in_specs=[pl.BlockSpec((blk, wide), lambda b, hh, i: (b * nq + i, q_col // heads + hh)),
                  pl.BlockSpec((seq, wide), lambda b, hh, i: (b, k_col // heads + hh)),
                  pl.BlockSpec((seq, wide), lambda b, hh, i: (b, v_col // heads + hh))],
        out_specs=pl.BlockSpec((blk, wide), lambda b, hh, i: (b * nq + i, hh)),
        out_shape=jax.ShapeDtypeStruct((batch * seq, n_heads * hd), BF16),
        compiler_params=_cparams(3),
        name="sb_attention",
    )(h, h, h)


def _rope_kernel(pos_ref, inv_ref, cos_ref, sin_ref):
    ang = pos_ref[...] * inv_ref[...]
    cos_ref[...] = jnp.cos(ang)
    sin_ref[...] = jnp.sin(ang)


def _rope_tables(pos_col, inv_row, tm=512):
    m = pos_col.shape[0]
    half = inv_row.shape[1]
    tm = min(tm, m)
    out = pl.BlockSpec((tm, half), lambda i: (i, 0))
    return pl.pallas_call(
        _rope_kernel,
        grid=(m // tm,),
        in_specs=[pl.BlockSpec((tm, 1), lambda i: (i, 0)), pl.BlockSpec((1, half), lambda i: (0, 0))],
        out_specs=[out, out],
        out_shape=[jax.ShapeDtypeStruct((m, half), F32)] * 2,
        compiler_params=_cparams(1),
        name="rope_tables",
    )(pos_col, inv_row)


def _ret_kernel(lg_ref, q_ref, k_ref, v_ref, gate_ref, cos_ref, sin_ref, gn_ref, o_ref, state_ref, *, blk):
    hh = pl.program_id(1)
    i = pl.program_id(2)
    d = q_ref.shape[1]
    half = d // 2
    lg = lg_ref[hh]

    @pl.when(i == 0)
    def _():
        state_ref[...] = jnp.zeros_like(state_ref)

    cos = cos_ref[...]
    sin = sin_ref[...]

    def rot(t_ref):
        t = t_ref[...].astype(F32)
        t1, t2 = t[:, :half], t[:, half:]
        return jnp.concatenate([t1 * cos - t2 * sin, t1 * sin + t2 * cos], axis=1)

    qr = rot(q_ref)
    kr = rot(k_ref)
    v = v_ref[...]
    n = lax.broadcasted_iota(jnp.int32, (blk, 1), 0).astype(F32)
    q_decay = jnp.exp((n + 1.0) * lg)
    k_decay = jnp.exp((blk - 1.0 - n) * lg)
    row = lax.broadcasted_iota(jnp.int32, (blk, blk), 0)
    col = lax.broadcasted_iota(jnp.int32, (blk, blk), 1)
    dist = jnp.abs(row - col).astype(F32)
    shift = CHUNK.bit_length() - 1
    seen = (col >> shift) <= (row >> shift)
    decay = jnp.where(seen, jnp.exp(dist * lg), 0.0)
    scores = _dot_nt(qr.astype(BF16), kr.astype(BF16)) * decay
    out = _dot(scores.astype(BF16), v) + _dot((qr * q_decay).astype(BF16), state_ref[...].astype(BF16))
    kv = lax.dot_general((kr * k_decay).astype(BF16), v, (((0,), (0,)), ((), ())), preferred_element_type=F32)
    state_ref[...] = jnp.exp(jnp.full((1, d), blk * lg, F32)) * state_ref[...] + kv
    mu = jnp.mean(out, axis=-1, keepdims=True)
    dv = out - mu
    var = jnp.mean(dv * dv, axis=-1, keepdims=True)
    y = dv * lax.rsqrt(var + LN_EPS) * gn_ref[...]
    g = gate_ref[...].astype(F32)
    o_ref[...] = (y * (g * _sigmoid(g))).astype(o_ref.dtype)


def _retention(h, cos, sin, log_g, gn, batch, seq, n_heads, d, blk=256):
    blk = min(blk, seq)
    assert blk % CHUNK == 0 and CHUNK & (CHUNK - 1) == 0
    nb = seq // blk
    hcol = lambda off: pl.BlockSpec((blk, d), lambda b, hh, i: (b * nb + i, off + hh))
    tab = pl.BlockSpec((blk, d // 2), lambda b, hh, i: (b * nb + i, 0))
    return pl.pallas_call(
        functools.partial(_ret_kernel, blk=blk),
        grid=(batch, n_heads, nb),
        in_specs=[pl.BlockSpec(memory_space=pltpu.SMEM),
                  hcol(0), hcol(n_heads), hcol(2 * n_heads), hcol(3 * n_heads), tab, tab,
                  pl.BlockSpec((1, d), lambda b, hh, i: (0, hh))],
        out_specs=pl.BlockSpec((blk, d), lambda b, hh, i: (b * nb + i, hh)),
        out_shape=jax.ShapeDtypeStruct((batch * seq, n_heads * d), BF16),
        scratch_shapes=[pltpu.VMEM((d, d), F32)],
        compiler_params=_cparams(3),
        name="retention",
    )(log_g, h, h, h, h, cos, sin, gn.reshape(1, n_heads * d))


def _router_kernel(x_ref, w_ref, b_ref, o_ref, *, n_exp, epg):
    logits = _dot(x_ref[...], w_ref[...]) + b_ref[...]
    lane = lax.broadcasted_iota(jnp.int32, logits.shape, 1)
    lmax = jnp.max(logits, axis=-1, keepdims=True)
    p = jnp.where(lane < n_exp, jnp.exp(logits - lmax), -1.0)
    p1 = jnp.max(p, axis=-1, keepdims=True)
    e1 = jnp.min(jnp.where(p == p1, lane, LANE), axis=-1, keepdims=True)
    shift = epg.bit_length() - 1
    in_group = (lane >> shift) == (e1 >> shift)
    cand = jnp.where(in_group, jnp.where(lane == e1, -1.0, p), -1.0)
    p2 = jnp.max(cand, axis=-1, keepdims=True)
    e2 = jnp.min(jnp.where(cand == p2, lane, LANE), axis=-1, keepdims=True)
    tot = p1 + p2
    o_ref[...] = jnp.where(lane == 0, e1.astype(F32),
                           jnp.where(lane == 1, e2.astype(F32),
                                     jnp.where(lane == 2, p1 / tot, jnp.where(lane == 3, p2 / tot, 0.0))))


def _router(xb, w_pad, b_pad, n_exp, tm=512):
    m, d = xb.shape
    tm = min(tm, m)
    epg = n_exp // N_GROUPS
    assert epg & (epg - 1) == 0 and n_exp <= LANE
    return pl.pallas_call(
        functools.partial(_router_kernel, n_exp=n_exp, epg=epg),
        grid=(m // tm,),
        in_specs=[pl.BlockSpec((tm, d), lambda i: (i, 0)),
                  pl.BlockSpec((d, LANE), lambda i: (0, 0)),
                  pl.BlockSpec((1, LANE), lambda i: (0, 0))],
        out_specs=pl.BlockSpec((tm, LANE), lambda i: (i, 0)),
        out_shape=jax.ShapeDtypeStruct((m, LANE), F32),
        compiler_params=_cparams(1),
        name="router",
    )(xb, w_pad, b_pad)


GATHER_UNROLL = 8


def _row_copy(src_hbm, row, dst_ref, r, sem):
    return pltpu.make_async_copy(src_hbm.at[pl.ds(row, 1), :], dst_ref.at[pl.ds(r, 1), :], sem)


def _gather_start(idx_ref, n_rows, src_hbm, dst_ref, sem):
    def issue(r, carry):
        _row_copy(src_hbm, idx_ref[0, 0, r], dst_ref, r, sem).start()
        return carry
    lax.fori_loop(0, n_rows, issue, 0, unroll=GATHER_UNROLL)


def _gather_wait(n_rows, src_hbm, dst_ref, sem):
    def drain(r, carry):
        _row_copy(src_hbm, 0, dst_ref, r, sem).wait()
        return carry
    lax.fori_loop(0, n_rows, drain, 0, unroll=GATHER_UNROLL)


def _moe_kernel(te_ref, tv_ref, cur_ref, nxt_ref, x_hbm, wg_ref, wu_ref, wd_ref, y_ref, xbuf, sem, *, t_rows):
    t = pl.program_id(0)
    slot = t % 2

    @pl.when(t == 0)
    def _():
        _gather_start(cur_ref, t_rows, x_hbm, xbuf.at[0], sem.at[0])

    @pl.when(jnp.logical_and(t + 1 < pl.num_programs(0), tv_ref[jnp.minimum(t + 1, pl.num_programs(0) - 1)] == 1))
    def _():
        _gather_start(nxt_ref, t_rows, x_hbm, xbuf.at[1 - slot], sem.at[1 - slot])

    @pl.when(tv_ref[t] == 1)
    def _():
        _gather_wait(t_rows, x_hbm, xbuf.at[slot], sem.at[slot])
        xb = xbuf[slot].astype(BF16)
        hg = _dot(xb, wg_ref[0, 0])
        hu = _dot(xb, wu_ref[0, 0])
        hidden = (hg * _sigmoid(hg) * hu).astype(BF16)
        y_ref[...] = _dot(hidden, wd_ref[0, 0])

    @pl.when(tv_ref[t] == 0)
    def _():
        y_ref[...] = jnp.zeros_like(y_ref)


def _moe_ffn(x, src_rows, tile_expert, tile_valid, w_gate, w_up, w_down, layer, t_rows):
    n_tiles = tile_expert.shape[0]
    d = x.shape[1]
    f = w_gate.shape[3]
    wspec = lambda a, b: pl.BlockSpec((1, 1, a, b), lambda t, te, tv: (layer, te[t], 0, 0))
    grid_spec = pltpu.PrefetchScalarGridSpec(
        num_scalar_prefetch=2,
        grid=(n_tiles,),
        in_specs=[pl.BlockSpec((1, 1, t_rows), lambda t, te, tv: (t, 0, 0), memory_space=pltpu.SMEM),
                  pl.BlockSpec((1, 1, t_rows), lambda t, te, tv: (jnp.minimum(t + 1, n_tiles - 1), 0, 0),
                               memory_space=pltpu.SMEM),
                  pl.BlockSpec(memory_space=pl.ANY),
                  wspec(d, f), wspec(d, f), wspec(f, d)],
        out_specs=pl.BlockSpec((t_rows, d), lambda t, te, tv: (t, 0)),
        scratch_shapes=[pltpu.VMEM((2, t_rows, d), F32), pltpu.SemaphoreType.DMA((2,))],
    )
    src3 = src_rows.reshape(n_tiles, 1, t_rows)
    return pl.pallas_call(
        functools.partial(_moe_kernel, t_rows=t_rows),
        grid_spec=grid_spec,
        out_shape=jax.ShapeDtypeStruct((n_tiles * t_rows, d), F32),
        compiler_params=_cparams(1),
        name="moe_ffn",
    )(tile_expert, tile_valid, src3, src3, x, w_gate, w_up, w_down)


def _combine_kernel(cur_ref, nxt_ref, route_ref, x_ref, y_hbm, g_ref, b_ref, of_ref, ob_ref, buf, sem, *, tm, alpha):
    i = pl.program_id(0)
    slot = i % 2

    @pl.when(i == 0)
    def _():
        _gather_start(cur_ref, 2 * tm, y_hbm, buf.at[0], sem.at[0])

    @pl.when(i + 1 < pl.num_programs(0))
    def _():
        _gather_start(nxt_ref, 2 * tm, y_hbm, buf.at[1 - slot], sem.at[1 - slot])

    _gather_wait(2 * tm, y_hbm, buf.at[slot], sem.at[slot])
    w1 = route_ref[:, 2:3]
    w2 = route_ref[:, 3:4]
    ffn = w1 * buf[slot, pl.ds(0, tm), :] + w2 * buf[slot, pl.ds(tm, tm), :]
    o = _layernorm_rows(alpha * x_ref[...] + ffn, g_ref[...], b_ref[...])
    of_ref[...] = o
    ob_ref[...] = o.astype(BF16)


def _moe_combine_ln(x, y, route, slots, g, b, alpha, tm=128):
    m, d = x.shape
    row = pl.BlockSpec((tm, d), lambda i: (i, 0))
    vec = pl.BlockSpec((1, d), lambda i: (0, 0))
    return pl.pallas_call(
        functools.partial(_combine_kernel, tm=tm, alpha=alpha),
        grid=(m // tm,),
        in_specs=[pl.BlockSpec((1, 1, 2 * tm), lambda i: (i, 0, 0), memory_space=pltpu.SMEM),
                  pl.BlockSpec((1, 1, 2 * tm), lambda i: (jnp.minimum(i + 1, m // tm - 1), 0, 0),
                               memory_space=pltpu.SMEM),
                  pl.BlockSpec((tm, LANE), lambda i: (i, 0)),
                  row,
                  pl.BlockSpec(memory_space=pl.ANY),
                  vec, vec],
        out_specs=[row, row],
        out_shape=[jax.ShapeDtypeStruct((m, d), F32), jax.ShapeDtypeStruct((m, d), BF16)],
        scratch_shapes=[pltpu.VMEM((2, 2 * tm, d), F32), pltpu.SemaphoreType.DMA((2,))],
        compiler_params=_cparams(1),
        name="moe_combine_ln",
    )(slots, slots, route, x, y, g.reshape(1, d), b.reshape(1, d))


def _routing_tables(route, n_exp, t_rows, tm):
    m = route.shape[0]
    e = route[:, 0:2].astype(jnp.int32)
    flat_e = e.reshape(-1)
    onehot = (flat_e[:, None] == jnp.arange(n_exp, dtype=jnp.int32)[None, :]).astype(jnp.int32)
    csum = jnp.cumsum(onehot, axis=0)
    counts = csum[-1]
    rank = jnp.take_along_axis(csum, flat_e[:, None], axis=1)[:, 0] - 1
    padded = ((counts + t_rows - 1) // t_rows) * t_rows
    ends = jnp.cumsum(padded)
    starts = ends - padded
    slot = starts[flat_e] + rank
    n_slots = 2 * m + n_exp * t_rows
    n_tiles = n_slots // t_rows
    src_rows = jnp.zeros((n_slots,), jnp.int32).at[slot].set(jnp.arange(2 * m, dtype=jnp.int32) // 2)
    tile_start = jnp.arange(n_tiles, dtype=jnp.int32) * t_rows
    tile_valid = (tile_start < ends[-1]).astype(jnp.int32)
    last_valid = jnp.maximum(ends[-1] - 1, 0)
    tile_expert = jnp.searchsorted(ends, jnp.minimum(tile_start, last_valid), side="right").astype(jnp.int32)
    tile_expert = jnp.minimum(tile_expert, n_exp - 1)
    slot2 = slot.reshape(m, 2)
    slots = jnp.concatenate([slot2[:, 0].reshape(m // tm, 1, tm), slot2[:, 1].reshape(m // tm, 1, tm)], axis=2)
    return src_rows, tile_expert, tile_valid, slots


def _ple_kernel(a_ref, w_ref, x_ref, p_ref, wp_ref, of_ref, ob_ref):
    gate = _sigmoid(_dot(a_ref[...], w_ref[0]))
    o = x_ref[...] + gate * _dot(p_ref[0], wp_ref[0])
    of_ref[...] = o
    ob_ref[...] = o.astype(BF16)


def _ple(xb, xf, p_all, w_gate_all, w_ple_all, layer, tm=1024, tn=512):
    m, d = xb.shape
    pd = p_all.shape[2]
    tm, tn = _tile(m, tm), _tile(d, tn)
    out = pl.BlockSpec((tm, tn), lambda i, j: (i, j))
    return pl.pallas_call(
        _ple_kernel,
        grid=(m // tm, d // tn),
        in_specs=[pl.BlockSpec((tm, d), lambda i, j: (i, 0)),
                  pl.BlockSpec((1, d, tn), lambda i, j: (layer, 0, j)),
                  out,
                  pl.BlockSpec((1, tm, pd), lambda i, j: (layer, i, 0)),
                  pl.BlockSpec((1, pd, tn), lambda i, j: (layer, 0, j))],
        out_specs=[out, out],
        out_shape=[jax.ShapeDtypeStruct((m, d), F32), jax.ShapeDtypeStruct((m, d), BF16)],
        compiler_params=_cparams(2),
        name="ple",
    )(xb, w_gate_all, xf, p_all, w_ple_all)


def _deinterleave_heads(w, n_heads):
    kd, width = w.shape
    d = width // n_heads
    w4 = w.reshape(kd, n_heads, d // 2, 2)
    return jnp.concatenate([w4[..., 0], w4[..., 1]], axis=-1).reshape(kd, width)


def kernel(x, p, positions, w_in_even, conv_w, conv_b, conv_ln_g, conv_ln_b, fox_f_bias, w_out_even, w_in_odd, ret_norm_g, w_out_odd, ln_mix_g, ln_mix_b, ln_ffn_g, ln_ffn_b, w_router, b_router, w_gate, w_up, w_down, w_ple, w_ple_gate):
    batch, seq, d_model = x.shape
    depth = ln_mix_g.shape[0]
    m = batch * seq
    gw = d_model // 2
    ch = conv_w.shape[2]
    fox_heads = fox_f_bias.shape[1]
    fox_hd = gw // fox_heads
    ret_hd = gw // RET_HEADS
    sb_hd = gw // SB_HEADS
    n_exp = w_router.shape[1]
    alpha = (2 * depth) ** 0.25
    moe_rows = min(256, m)
    comb_rows = min(128, m)
    fox_blk = min(512, seq)

    xf = x.reshape(m, d_model)
    xb = xf.astype(BF16)
    p_b = p.reshape(depth, m, p.shape[-1]).astype(BF16)
    w_ple_b = w_ple.astype(BF16)
    w_ple_gate_b = w_ple_gate.astype(BF16)
    w_gate_b, w_up_b, w_down_b = w_gate.astype(BF16), w_up.astype(BF16), w_down.astype(BF16)
    w_router_pad = jnp.zeros((d_model, LANE), F32).at[:, :n_exp].set(w_router).astype(BF16)
    b_router_pad = jnp.full((1, LANE), NEG, F32).at[0, :n_exp].set(b_router)

    inv = ROPE_BASE ** (-jnp.arange(0, ret_hd, 2, dtype=F32) / ret_hd)
    log_g = jnp.log1p(-jnp.exp2(-5.0 - jnp.arange(RET_HEADS, dtype=F32)))
    cos, sin = _rope_tables(positions.reshape(m, 1).astype(F32), inv.reshape(1, ret_hd // 2))

    for i in range(depth):
        j = i // 2
        if i % 2 == 0:
            w_in = w_in_even[j]
            n_main = 2 * ch + 3 * gw
            scale_cols = jnp.ones((n_main,), F32).at[2 * ch:2 * ch + gw].set(fox_hd ** -0.5)
            h = _matmul(xb, (w_in[:, :n_main] * scale_cols).astype(BF16), BF16)
            w_f = jnp.zeros((d_model, LANE), F32).at[:, :fox_heads].set(w_in[:, n_main:]).astype(BF16)
            f_logit = _matmul(xb, w_f, F32)[:, :fox_heads]
            f_rows = f_logit.reshape(batch, seq, fox_heads).transpose(0, 2, 1).reshape(batch * fox_heads, seq)
            bias_rows = jnp.tile(fox_f_bias[j], batch).reshape(batch * fox_heads, 1)
            cum = _fox_cum(f_rows, bias_rows).reshape(batch * fox_heads, seq // fox_blk, 1, fox_blk)
            conv_out = _conv_module(h, batch, seq, ch, conv_w[j], conv_b[j], conv_ln_g[j], conv_ln_b[j])
            q_col = 2 * ch // fox_hd
            fox_out = _fox_attention(h, cum, batch, seq, fox_heads, fox_hd,
                                     q_col, q_col + fox_heads, q_col + 2 * fox_heads, blk=fox_blk)
            mixed = _matmul_cat(conv_out, fox_out, w_out_even[j].astype(BF16))
        else:
            w_in = w_in_odd[j]
            w_rq = _deinterleave_heads(w_in[:, :gw], RET_HEADS) * (ret_hd ** -0.5)
            w_rk = _deinterleave_heads(w_in[:, gw:2 * gw], RET_HEADS)
            w_sq = w_in[:, 4 * gw:5 * gw] * (sb_hd ** -0.5)
            w_odd = jnp.concatenate([w_rq, w_rk, w_in[:, 2 * gw:4 * gw], w_sq, w_in[:, 5 * gw:]], axis=1)
            h = _matmul(xb, w_odd.astype(BF16), BF16)
            ret_out = _retention(h, cos, sin, log_g, ret_norm_g[j], batch, seq, RET_HEADS, ret_hd)
            s_col = 4 * gw // sb_hd
            sb_out = _sb_attention(h, batch, seq, SB_HEADS, sb_hd, s_col, s_col + SB_HEADS, s_col + 2 * SB_HEADS)
            mixed = _matmul_cat(ret_out, sb_out, w_out_odd[j].astype(BF16))
        xf, xb = _res_ln(xf, mixed, ln_mix_g[i], ln_mix_b[i], alpha)
        route = _router(xb, w_router_pad, b_router_pad, n_exp)
        src_rows, tile_expert, tile_valid, slots = _routing_tables(route, n_exp, moe_rows, comb_rows)
        y = _moe_ffn(xf, src_rows, tile_expert, tile_valid, w_gate_b, w_up_b, w_down_b, i, moe_rows)
        xf, xb = _moe_combine_ln(xf, y, route, slots, ln_ffn_g[i], ln_ffn_b[i], alpha, tm=comb_rows)
        xf, xb = _ple(xb, xf, p_b, w_ple_gate_b, w_ple_b, i)
    return xf.reshape(batch, seq, d_model)
```

```python
import functools

import jax
import jax.numpy as jnp
from jax import lax
from jax.experimental import pallas as pl
from jax.experimental.pallas import tpu as pltpu

F32 = jnp.float32
BF16 = jnp.bfloat16

CHUNK = 64
N_GROUPS = 4
RET_HEADS = 8
SB_HEADS = 16
ROPE_BASE = 10000.0
LN_EPS = 1e-5
NEG = -1e30

LANE = 128
SUBLANE = 8
VMEM_LIMIT_BYTES = 56 * 1024 * 1024


def _cparams(n_axes):
    return pltpu.CompilerParams(dimension_semantics=("arbitrary",) * n_axes,
                                vmem_limit_bytes=VMEM_LIMIT_BYTES)


def _sigmoid(x):
    return 1.0 / (1.0 + jnp.exp(-x))


def _dot(a, b):
    return jnp.dot(a, b, preferred_element_type=F32)


def _dot_nt(a, b):
    return lax.dot_general(a, b, (((1,), (1,)), ((), ())), preferred_element_type=F32)


def _mm_kernel(a_ref, b_ref, o_ref):
    o_ref[...] = _dot(a_ref[...], b_ref[...]).astype(o_ref.dtype)


def _tile(n, pref):
    t = min(pref, n)
    while n % t or (t % LANE and t != n):
        t -= LANE if t % LANE == 0 else t % LANE
    return t


def _matmul(a, b, out_dtype, tm=1024, tn=512):
    m, kd = a.shape
    n = b.shape[1]
    tm, tn = _tile(m, tm), _tile(n, tn)
    return pl.pallas_call(
        _mm_kernel,
        grid=(m // tm, n // tn),
        in_specs=[pl.BlockSpec((tm, kd), lambda i, j: (i, 0)),
                  pl.BlockSpec((kd, tn), lambda i, j: (0, j))],
        out_specs=pl.BlockSpec((tm, tn), lambda i, j: (i, j)),
        out_shape=jax.ShapeDtypeStruct((m, n), out_dtype),
        compiler_params=_cparams(2),
        name="matmul",
    )(a, b)


def _proj_in_kernel(a_ref, w_ref, s_ref, o_ref, wb_ref):
    @pl.when(pl.program_id(1) == 0)
    def _():
        wb_ref[...] = (w_ref[0] * s_ref[...]).astype(BF16)

    o_ref[...] = _dot(a_ref[...], wb_ref[...]).astype(o_ref.dtype)


def _proj_in(a, w_all, layer, n_cols, col_scale, tm=1024, tn=512):
    m, kd = a.shape
    tm, tn = _tile(m, tm), _tile(n_cols, tn)
    return pl.pallas_call(
        _proj_in_kernel,
        grid=(n_cols // tn, m // tm),
        in_specs=[pl.BlockSpec((tm, kd), lambda n, i: (i, 0)),
                  pl.BlockSpec((1, kd, tn), lambda n, i: (layer, 0, n)),
                  pl.BlockSpec((1, tn), lambda n, i: (0, n))],
        out_specs=pl.BlockSpec((tm, tn), lambda n, i: (i, n)),
        out_shape=jax.ShapeDtypeStruct((m, n_cols), BF16),
        scratch_shapes=[pltpu.VMEM((kd, tn), BF16)],
        compiler_params=_cparams(2),
        name="proj_in",
    )(a, w_all, col_scale.reshape(1, n_cols))


def _proj_out_kernel(a1_ref, a2_ref, w1_ref, w2_ref, o_ref, wb1_ref, wb2_ref):
    @pl.when(pl.program_id(1) == 0)
    def _():
        wb1_ref[...] = w1_ref[0].astype(BF16)
        wb2_ref[...] = w2_ref[0].astype(BF16)

    o_ref[...] = _dot(a1_ref[...], wb1_ref[...]) + _dot(a2_ref[...], wb2_ref[...])


def _proj_out(a1, a2, w_all, layer, tm=1024, tn=512):
    m, k1 = a1.shape
    n = w_all.shape[2]
    assert a2.shape[1] == k1 and w_all.shape[1] == 2 * k1
    tm, tn = _tile(m, tm), _tile(n, tn)
    act = pl.BlockSpec((tm, k1), lambda c, i: (i, 0))
    return pl.pallas_call(
        _proj_out_kernel,
        grid=(n // tn, m // tm),
        in_specs=[act, act,
                  pl.BlockSpec((1, k1, tn), lambda c, i: (layer, 0, c)),
                  pl.BlockSpec((1, k1, tn), lambda c, i: (layer, 1, c))],
        out_specs=pl.BlockSpec((tm, tn), lambda c, i: (i, c)),
        out_shape=jax.ShapeDtypeStruct((m, n), F32),
        scratch_shapes=[pltpu.VMEM((k1, tn), BF16), pltpu.VMEM((k1, tn), BF16)],
        compiler_params=_cparams(2),
        name="proj_out",
    )(a1, a2, w_all, w_all)


def _layernorm_rows(t, g, b):
    mu = jnp.mean(t, axis=-1, keepdims=True)
    d = t - mu
    var = jnp.mean(d * d, axis=-1, keepdims=True)
    return d * lax.rsqrt(var + LN_EPS) * g + b


def _res_ln_kernel(x_ref, y_ref, g_ref, b_ref, of_ref, ob_ref, *, alpha):
    o = _layernorm_rows(alpha * x_ref[...] + y_ref[...], g_ref[...], b_ref[...])
    of_ref[...] = o
    ob_ref[...] = o.astype(BF16)


def _res_ln(x, y, g, b, alpha, tm=256):
    m, d = x.shape
    tm = min(tm, m)
    row = pl.BlockSpec((tm, d), lambda i: (i, 0))
    vec = pl.BlockSpec((1, d), lambda i: (0, 0))
    return pl.pallas_call(
        functools.partial(_res_ln_kernel, alpha=alpha),
        grid=(m // tm,),
        in_specs=[row, row, vec, vec],
        out_specs=[row, row],
        out_shape=[jax.ShapeDtypeStruct((m, d), F32), jax.ShapeDtypeStruct((m, d), BF16)],
        compiler_params=_cparams(1),
        name="res_ln",
    )(x, y, g.reshape(1, d), b.reshape(1, d))


def _conv_kernel(a_ref, g_ref, ap_ref, gp_ref, cw_ref, cb_ref, lg_ref, lb_ref, o_ref, buf_ref, y_ref,
                 *, t_rows, halo, width):
    i = pl.program_id(1)
    ch = a_ref.shape[1]
    buf_ref[pl.ds(halo, t_rows), :] = a_ref[...].astype(F32) * _sigmoid(g_ref[...].astype(F32))
    ap = ap_ref[pl.ds(t_rows - halo, halo), :].astype(F32)
    gp = gp_ref[pl.ds(t_rows - halo, halo), :].astype(F32)
    tail = ap * _sigmoid(gp)
    buf_ref[pl.ds(0, halo), :] = jnp.where(i > 0, tail, 0.0)
    base = halo - (width - 1)
    for c0 in range(0, ch, LANE):
        acc = jnp.zeros((t_rows, LANE), F32) + cb_ref[:, c0:c0 + LANE]
        for r in range(SUBLANE):
            taps = [w for w in range(width) if (base + w) % SUBLANE == r]
            if not taps:
                continue
            rows = t_rows + (SUBLANE if r else 0)
            z = None
            for w in taps:
                term = buf_ref[pl.ds(base + w - r, rows), c0:c0 + LANE] * cw_ref[w:w + 1, c0:c0 + LANE]
                z = term if z is None else z + term
            acc = acc + z[r:r + t_rows, :]
        y_ref[:, c0:c0 + LANE] = acc
    y = _layernorm_rows(y_ref[...], lg_ref[...], lb_ref[...])
    o_ref[...] = (y * _sigmoid(y)).astype(o_ref.dtype)


def _conv_module(h, batch, seq, ch, conv_w, conv_b, ln_g, ln_b, t_rows=256, halo=32):
    width = conv_w.shape[0]
    t_rows = min(t_rows, seq)
    assert width - 1 <= halo <= t_rows and seq % t_rows == 0 and halo % SUBLANE == 0
    nt = seq // t_rows
    cur = lambda col: pl.BlockSpec((t_rows, ch), lambda b, i: (b * nt + i, col))
    prev = lambda col: pl.BlockSpec((t_rows, ch), lambda b, i: (b * nt + jnp.maximum(i - 1, 0), col))
    vec = pl.BlockSpec((1, ch), lambda b, i: (0, 0))
    return pl.pallas_call(
        functools.partial(_conv_kernel, t_rows=t_rows, halo=halo, width=width),
        grid=(batch, nt),
        in_specs=[cur(0), cur(1), prev(0), prev(1),
                  pl.BlockSpec((width, ch), lambda b, i: (0, 0)), vec, vec, vec],
        out_specs=pl.BlockSpec((t_rows, ch), lambda b, i: (b * nt + i, 0)),
        out_shape=jax.ShapeDtypeStruct((batch * seq, ch), BF16),
        scratch_shapes=[pltpu.VMEM((t_rows + halo, ch), F32), pltpu.VMEM((t_rows, ch), F32)],
        compiler_params=_cparams(2),
        name="conv_module",
    )(h, h, h, h, conv_w, conv_b.reshape(1, ch), ln_g.reshape(1, ch), ln_b.reshape(1, ch))


def _cum_kernel(f_ref, b_ref, o_ref):
    x = f_ref[...] + b_ref[...]
    ls = jnp.minimum(x, 0.0) - jnp.log(1.0 + jnp.exp(-jnp.abs(x)))
    lane = lax.broadcasted_iota(jnp.int32, ls.shape, 1)
    sh = 1
    while sh < ls.shape[1]:
        ls = ls + jnp.where(lane >= sh, pltpu.roll(ls, sh, 1), 0.0)
        sh *= 2
    o_ref[...] = ls


def _fox_cum(f_rows, bias_rows):
    r, s = f_rows.shape
    return pl.pallas_call(
        _cum_kernel,
        grid=(1,),
        in_specs=[pl.BlockSpec((r, s), lambda i: (0, 0)), pl.BlockSpec((r, 1), lambda i: (0, 0))],
        out_specs=pl.BlockSpec((r, s), lambda i: (0, 0)),
        out_shape=jax.ShapeDtypeStruct((r, s), F32),
        compiler_params=_cparams(1),
        name="fox_cum",
    )(f_rows, bias_rows)


def _fox_kernel(q_ref, k_ref, v_ref, c_ref, o_ref, *, blk):
    i = pl.program_id(2)
    hd = q_ref.shape[1]
    q = q_ref[...]
    row = lax.broadcasted_iota(jnp.int32, (blk, blk), 0)
    col = lax.broadcasted_iota(jnp.int32, (blk, blk), 1)

    def step(kj, carry, masked):
        m, l, acc = carry
        start = pl.multiple_of(kj * blk, blk)
        k = k_ref[pl.ds(start, blk), :]
        v = v_ref[pl.ds(start, blk), :]
        s = _dot_nt(q, k) - c_ref[0, kj]
        if masked:
            s = jnp.where(col <= row, s, NEG)
        m_new = jnp.maximum(m, jnp.max(s, axis=-1, keepdims=True))
        p = jnp.exp(s - m_new)
        alpha = jnp.exp(m - m_new)
        l = alpha * l + jnp.sum(p, axis=-1, keepdims=True)
        acc = alpha * acc + _dot(p.astype(BF16), v)
        return m_new, l, acc

    init = (jnp.full((blk, 1), NEG, F32), jnp.zeros((blk, 1), F32), jnp.zeros((blk, hd), F32))
    carry = lax.fori_loop(0, i, lambda kj, c: step(kj, c, False), init)
    _, l, acc = step(i, carry, True)
    o_ref[...] = (acc / l).astype(o_ref.dtype)


def _fox_attention(h, cum, batch, seq, n_heads, hd, q_col, k_col, v_col, blk=512):
    blk = min(blk, seq)
    nq = seq // blk
    return pl.pallas_call(
        functools.partial(_fox_kernel, blk=blk),
        grid=(batch, n_heads, nq),
        in_specs=[pl.BlockSpec((blk, hd), lambda b, hh, i: (b * nq + i, q_col + hh)),
                  pl.BlockSpec((seq, hd), lambda b, hh, i: (b, k_col + hh)),
                  pl.BlockSpec((seq, hd), lambda b, hh, i: (b, v_col + hh)),
                  pl.BlockSpec((1, nq, 1, blk), lambda b, hh, i: (b * n_heads + hh, 0, 0, 0))],
        out_specs=pl.BlockSpec((blk, hd), lambda b, hh, i: (b * nq + i, hh)),
        out_shape=jax.ShapeDtypeStruct((batch * seq, n_heads * hd), BF16),
        compiler_params=_cparams(3),
        name="fox_attention",
    )(h, h, h, cum)


SB_DEAD = 105.0


def _sb_kernel(q_ref, k_ref, v_ref, o_ref, *, blk, heads, hd):
    i = pl.program_id(2)
    jr = lax.broadcasted_iota(jnp.int32, (blk, blk), 0)
    jc = lax.broadcasted_iota(jnp.int32, (blk, blk), 1)
    upper = jnp.where(jr > jc, 1.0, 0.0).astype(BF16)
    strict = jc < jr

    def block(hh, start, run, acc, masked):
        cols = slice(hh * hd, (hh + 1) * hd)
        z = _dot_nt(q_ref[:, cols], k_ref[pl.ds(start, blk), cols])
        sp = jnp.maximum(z, 0.0) + jnp.log(1.0 + jnp.exp(-jnp.abs(z)))
        log_1m = -sp
        log_b = z - sp
        if masked:
            log_1m = jnp.where(strict, log_1m, 0.0)
        hi = log_1m.astype(BF16)
        lo = (log_1m - hi.astype(F32)).astype(BF16)
        later = _dot(hi, upper) + _dot(lo, upper)
        w = jnp.exp(log_b + later + run)
        if masked:
            w = jnp.where(strict, w, 0.0)
        acc = acc + _dot(w.astype(BF16), v_ref[pl.ds(start, blk), cols])
        return run + later[:, 0:1] + log_1m[:, 0:1], acc

    def step(kj, carry, masked):
        start = pl.multiple_of(kj * blk, blk)
        return tuple(block(hh, start, *carry[hh], masked) for hh in range(heads))

    def live(carry):
        return functools.reduce(jnp.maximum, [jnp.max(run) for run, _ in carry])

    carry = tuple((jnp.zeros((blk, 1), F32), jnp.zeros((blk, hd), F32)) for _ in range(heads))
    carry = step(i, carry, True)

    def cond(state):
        t, top, _ = state
        return jnp.logical_and(t < i, top > -SB_DEAD)

    def body(state):
        t, _, c = state
        c = step(i - 1 - t, c, False)
        return t + 1, live(c), c

    _, _, carry = lax.while_loop(cond, body, (jnp.int32(0), live(carry), carry))
    for hh in range(heads):
        o_ref[:, hh * hd:(hh + 1) * hd] = carry[hh][1].astype(o_ref.dtype)


def _sb_attention(h, batch, seq, n_heads, hd, q_col, k_col, v_col, blk=256, heads=2):
    blk = min(blk, seq)
    nq = seq // blk
    assert n_heads % heads == 0 and q_col % heads == 0 and k_col % heads == 0 and v_col % heads == 0
    wide = heads * hd
    return pl.pallas_call(
        functools.partial(_sb_kernel, blk=blk, heads=heads, hd=hd),
        grid=(batch, n_heads // heads, nq),
        in_specs=[pl.BlockSpec((blk, wide), lambda b, hh, i: (b * nq + i, q_col // heads + hh)),
                  pl.BlockSpec((seq, wide), lambda b, hh, i: (b, k_col // heads + hh)),
                  pl.BlockSpec((seq, wide), lambda b, hh, i: (b, v_col // heads + hh))],
        out_specs=pl.BlockSpec((blk, wide), lambda b, hh, i: (b * nq + i, hh)),
        out_shape=jax.ShapeDtypeStruct((batch * seq, n_heads * hd), BF16),
        compiler_params=_cparams(3),
        name="sb_attention",
    )(h, h, h)


def _rope_kernel(pos_ref, inv_ref, cos_ref, sin_ref):
    ang = pos_ref[...] * inv_ref[...]
    lane = lax.broadcasted_iota(jnp.int32, ang.shape, 1)
    cos_ref[...] = jnp.cos(ang)
    sin_ref[...] = jnp.where((lane & 1) == 0, -jnp.sin(ang), jnp.sin(ang))


def _rope_tables(pos_col, inv_row, tm=512):
    m = pos_col.shape[0]
    half = inv_row.shape[1]
    tm = min(tm, m)
    out = pl.BlockSpec((tm, half), lambda i: (i, 0))
    return pl.pallas_call(
        _rope_kernel,
        grid=(m // tm,),
        in_specs=[pl.BlockSpec((tm, 1), lambda i: (i, 0)), pl.BlockSpec((1, half), lambda i: (0, 0))],
        out_specs=[out, out],
        out_shape=[jax.ShapeDtypeStruct((m, half), F32)] * 2,
        compiler_params=_cparams(1),
        name="rope_tables",
    )(pos_col, inv_row)


def _ret_kernel(lg_ref, q_ref, k_ref, v_ref, gate_ref, cos_ref, sin_ref, gn_ref, o_ref, state_ref, *, blk):
    hh = pl.program_id(1)
    i = pl.program_id(2)
    d = q_ref.shape[1]
    lg = lg_ref[hh]

    @pl.when(i == 0)
    def _():
        state_ref[...] = jnp.zeros_like(state_ref)

    cos = cos_ref[...]
    sin = sin_ref[...]
    even = (lax.broadcasted_iota(jnp.int32, (blk, d), 1) & 1) == 0

    def rot(t_ref):
        t = t_ref[...].astype(F32)
        partner = jnp.where(even, pltpu.roll(t, d - 1, 1), pltpu.roll(t, 1, 1))
        return t * cos + partner * sin

    qr = rot(q_ref)
    kr = rot(k_ref)
    v = v_ref[...]
    n = lax.broadcasted_iota(jnp.int32, (blk, 1), 0).astype(F32)
    q_decay = jnp.exp((n + 1.0) * lg)
    k_decay = jnp.exp((blk - 1.0 - n) * lg)
    row = lax.broadcasted_iota(jnp.int32, (blk, blk), 0)
    col = lax.broadcasted_iota(jnp.int32, (blk, blk), 1)
    dist = jnp.abs(row - col).astype(F32)
    shift = CHUNK.bit_length() - 1
    seen = (col >> shift) <= (row >> shift)
    decay = jnp.where(seen, jnp.exp(dist * lg), 0.0)
    scores = _dot_nt(qr.astype(BF16), kr.astype(BF16)) * decay
    out = _dot(scores.astype(BF16), v) + _dot((qr * q_decay).astype(BF16), state_ref[...].astype(BF16))
    kv = lax.dot_general((kr * k_decay).astype(BF16), v, (((0,), (0,)), ((), ())), preferred_element_type=F32)
    state_ref[...] = jnp.exp(jnp.full((1, d), blk * lg, F32)) * state_ref[...] + kv
    mu = jnp.mean(out, axis=-1, keepdims=True)
    dv = out - mu
    var = jnp.mean(dv * dv, axis=-1, keepdims=True)
    y = dv * lax.rsqrt(var + LN_EPS) * gn_ref[...]
    g = gate_ref[...].astype(F32)
    o_ref[...] = (y * (g * _sigmoid(g))).astype(o_ref.dtype)


def _retention(h, cos, sin, log_g, gn, batch, seq, n_heads, d, blk=256):
    blk = min(blk, seq)
    assert blk % CHUNK == 0 and CHUNK & (CHUNK - 1) == 0
    nb = seq // blk
    hcol = lambda off: pl.BlockSpec((blk, d), lambda b, hh, i: (b * nb + i, off + hh))
    tab = pl.BlockSpec((blk, d), lambda b, hh, i: (b * nb + i, 0))
    return pl.pallas_call(
        functools.partial(_ret_kernel, blk=blk),
        grid=(batch, n_heads, nb),
        in_specs=[pl.BlockSpec(memory_space=pltpu.SMEM),
                  hcol(0), hcol(n_heads), hcol(2 * n_heads), hcol(3 * n_heads), tab, tab,
                  pl.BlockSpec((1, d), lambda b, hh, i: (0, hh))],
        out_specs=pl.BlockSpec((blk, d), lambda b, hh, i: (b * nb + i, hh)),
        out_shape=jax.ShapeDtypeStruct((batch * seq, n_heads * d), BF16),
        scratch_shapes=[pltpu.VMEM((d, d), F32)],
        compiler_params=_cparams(3),
        name="retention",
    )(log_g, h, h, h, h, cos, sin, gn.reshape(1, n_heads * d))


def _router_kernel(x_ref, w_ref, b_ref, o_ref, *, n_exp, epg):
    logits = _dot(x_ref[...], w_ref[...]) + b_ref[...]
    lane = lax.broadcasted_iota(jnp.int32, logits.shape, 1)
    lmax = jnp.max(logits, axis=-1, keepdims=True)
    p = jnp.where(lane < n_exp, jnp.exp(logits - lmax), -1.0)
    p1 = jnp.max(p, axis=-1, keepdims=True)
    e1 = jnp.min(jnp.where(p == p1, lane, LANE), axis=-1, keepdims=True)
    shift = epg.bit_length() - 1
    in_group = (lane >> shift) == (e1 >> shift)
    cand = jnp.where(in_group, jnp.where(lane == e1, -1.0, p), -1.0)
    p2 = jnp.max(cand, axis=-1, keepdims=True)
    e2 = jnp.min(jnp.where(cand == p2, lane, LANE), axis=-1, keepdims=True)
    tot = p1 + p2
    o_ref[...] = jnp.where(lane == 0, e1.astype(F32),
                           jnp.where(lane == 1, e2.astype(F32),
                                     jnp.where(lane == 2, p1 / tot, jnp.where(lane == 3, p2 / tot, 0.0))))


def _router(xb, w_pad, b_pad, n_exp, tm=512):
    m, d = xb.shape
    tm = min(tm, m)
    epg = n_exp // N_GROUPS
    assert epg & (epg - 1) == 0 and n_exp <= LANE
    return pl.pallas_call(
        functools.partial(_router_kernel, n_exp=n_exp, epg=epg),
        grid=(m // tm,),
        in_specs=[pl.BlockSpec((tm, d), lambda i: (i, 0)),
                  pl.BlockSpec((d, LANE), lambda i: (0, 0)),
                  pl.BlockSpec((1, LANE), lambda i: (0, 0))],
        out_specs=pl.BlockSpec((tm, LANE), lambda i: (i, 0)),
        out_shape=jax.ShapeDtypeStruct((m, LANE), F32),
        compiler_params=_cparams(1),
        name="router",
    )(xb, w_pad, b_pad)


GATHER_UNROLL = 8


def _row_copy(src_hbm, row, dst_ref, r, sem):
    return pltpu.make_async_copy(src_hbm.at[pl.ds(row, 1), :], dst_ref.at[pl.ds(r, 1), :], sem)


def _gather_start(idx_ref, n_rows, src_hbm, dst_ref, sem):
    def issue(r, carry):
        _row_copy(src_hbm, idx_ref[0, 0, r], dst_ref, r, sem).start()
        return carry
    lax.fori_loop(0, n_rows, issue, 0, unroll=GATHER_UNROLL)


def _gather_wait(n_rows, src_hbm, dst_ref, sem):
    def drain(r, carry):
        _row_copy(src_hbm, 0, dst_ref, r, sem).wait()
        return carry
    lax.fori_loop(0, n_rows, drain, 0, unroll=GATHER_UNROLL)


def _moe_kernel(te_ref, tv_ref, cur_ref, nxt_ref, x_hbm, wg_ref, wu_ref, wd_ref, y_ref, xbuf, sem, *, t_rows):
    t = pl.program_id(0)
    slot = t % 2

    @pl.when(t == 0)
    def _():
        _gather_start(cur_ref, t_rows, x_hbm, xbuf.at[0], sem.at[0])

    @pl.when(jnp.logical_and(t + 1 < pl.num_programs(0), tv_ref[jnp.minimum(t + 1, pl.num_programs(0) - 1)] == 1))
    def _():
        _gather_start(nxt_ref, t_rows, x_hbm, xbuf.at[1 - slot], sem.at[1 - slot])

    @pl.when(tv_ref[t] == 1)
    def _():
        _gather_wait(t_rows, x_hbm, xbuf.at[slot], sem.at[slot])
        xb = xbuf[slot].astype(BF16)
        hg = _dot(xb, wg_ref[0, 0])
        hu = _dot(xb, wu_ref[0, 0])
        hidden = (hg * _sigmoid(hg) * hu).astype(BF16)
        y_ref[...] = _dot(hidden, wd_ref[0, 0])

    @pl.when(tv_ref[t] == 0)
    def _():
        y_ref[...] = jnp.zeros_like(y_ref)


def _moe_ffn(x, src_rows, tile_expert, tile_valid, w_gate, w_up, w_down, layer, t_rows):
    n_tiles = tile_expert.shape[0]
    d = x.shape[1]
    f = w_gate.shape[3]
    wspec = lambda a, b: pl.BlockSpec((1, 1, a, b), lambda t, te, tv: (layer, te[t], 0, 0))
    grid_spec = pltpu.PrefetchScalarGridSpec(
        num_scalar_prefetch=2,
        grid=(n_tiles,),
        in_specs=[pl.BlockSpec((1, 1, t_rows), lambda t, te, tv: (t, 0, 0), memory_space=pltpu.SMEM),
                  pl.BlockSpec((1, 1, t_rows), lambda t, te, tv: (jnp.minimum(t + 1, n_tiles - 1), 0, 0),
                               memory_space=pltpu.SMEM),
                  pl.BlockSpec(memory_space=pl.ANY),
                  wspec(d, f), wspec(d, f), wspec(f, d)],
        out_specs=pl.BlockSpec((t_rows, d), lambda t, te, tv: (t, 0)),
        scratch_shapes=[pltpu.VMEM((2, t_rows, d), F32), pltpu.SemaphoreType.DMA((2,))],
    )
    src3 = src_rows.reshape(n_tiles, 1, t_rows)
    return pl.pallas_call(
        functools.partial(_moe_kernel, t_rows=t_rows),
        grid_spec=grid_spec,
        out_shape=jax.ShapeDtypeStruct((n_tiles * t_rows, d), F32),
        compiler_params=_cparams(1),
        name="moe_ffn",
    )(tile_expert, tile_valid, src3, src3, x, w_gate, w_up, w_down)


def _combine_kernel(cur_ref, nxt_ref, route_ref, x_ref, y_hbm, g_ref, b_ref, of_ref, ob_ref, buf, sem, *, tm, alpha):
    i = pl.program_id(0)
    slot = i % 2

    @pl.when(i == 0)
    def _():
        _gather_start(cur_ref, 2 * tm, y_hbm, buf.at[0], sem.at[0])

    @pl.when(i + 1 < pl.num_programs(0))
    def _():
        _gather_start(nxt_ref, 2 * tm, y_hbm, buf.at[1 - slot], sem.at[1 - slot])

    _gather_wait(2 * tm, y_hbm, buf.at[slot], sem.at[slot])
    w1 = route_ref[:, 2:3]
    w2 = route_ref[:, 3:4]
    ffn = w1 * buf[slot, pl.ds(0, tm), :] + w2 * buf[slot, pl.ds(tm, tm), :]
    o = _layernorm_rows(alpha * x_ref[...] + ffn, g_ref[...], b_ref[...])
    of_ref[...] = o
    ob_ref[...] = o.astype(BF16)


def _moe_combine_ln(x, y, route, slots, g, b, alpha, tm=128):
    m, d = x.shape
    row = pl.BlockSpec((tm, d), lambda i: (i, 0))
    vec = pl.BlockSpec((1, d), lambda i: (0, 0))
    return pl.pallas_call(
        functools.partial(_combine_kernel, tm=tm, alpha=alpha),
        grid=(m // tm,),
        in_specs=[pl.BlockSpec((1, 1, 2 * tm), lambda i: (i, 0, 0), memory_space=pltpu.SMEM),
                  pl.BlockSpec((1, 1, 2 * tm), lambda i: (jnp.minimum(i + 1, m // tm - 1), 0, 0),
                               memory_space=pltpu.SMEM),
                  pl.BlockSpec((tm, LANE), lambda i: (i, 0)),
                  row,
                  pl.BlockSpec(memory_space=pl.ANY),
                  vec, vec],
        out_specs=[row, row],
        out_shape=[jax.ShapeDtypeStruct((m, d), F32), jax.ShapeDtypeStruct((m, d), BF16)],
        scratch_shapes=[pltpu.VMEM((2, 2 * tm, d), F32), pltpu.SemaphoreType.DMA((2,))],
        compiler_params=_cparams(1),
        name="moe_combine_ln",
    )(slots, slots, route, x, y, g.reshape(1, d), b.reshape(1, d))


def _slot_kernel(route_ref, slot_ref, ends_ref, cnt_ref, run_ref, start_ref, *, tb, t_rows):
    ph = pl.program_id(0)
    i = pl.program_id(1)
    r = route_ref[...]
    lane = lax.broadcasted_iota(jnp.int32, (tb, LANE), 1)
    oh1 = lane == r[:, 0:1].astype(jnp.int32)
    oh2 = lane == r[:, 1:2].astype(jnp.int32)
    both = jnp.where(oh1, 1.0, 0.0) + jnp.where(oh2, 1.0, 0.0)
    col_sum = jnp.sum(both, axis=0, keepdims=True)

    @pl.when(jnp.logical_and(ph == 0, i == 0))
    def _():
        cnt_ref[...] = jnp.zeros_like(cnt_ref)

    @pl.when(ph == 0)
    def _():
        cnt_ref[...] = cnt_ref[...] + col_sum

    @pl.when(jnp.logical_and(ph == 1, i == 0))
    def _():
        shift = t_rows.bit_length() - 1
        padded = ((cnt_ref[...].astype(jnp.int32) + (t_rows - 1)) >> shift) << shift
        lane8 = lax.broadcasted_iota(jnp.int32, padded.shape, 1)
        ends = padded
        sh = 1
        while sh < LANE:
            ends = ends + jnp.where(lane8 >= sh, pltpu.roll(ends, sh, 1), 0)
            sh *= 2
        start_ref[...] = (ends - padded).astype(F32)
        ends_ref[...] = ends.astype(F32)
        run_ref[...] = jnp.zeros_like(run_ref)

    @pl.when(ph == 1)
    def _():
        tr = lax.broadcasted_iota(jnp.int32, (tb, tb), 0)
        tc = lax.broadcasted_iota(jnp.int32, (tb, tb), 1)
        earlier = jnp.where(tr > tc, 1.0, 0.0).astype(BF16)
        before = _dot(earlier, both.astype(BF16)) + run_ref[0:1, :] + start_ref[0:1, :]
        s1 = jnp.sum(jnp.where(oh1, before, 0.0), axis=-1, keepdims=True)
        s2 = jnp.sum(jnp.where(oh2, before, 0.0), axis=-1, keepdims=True)
        slot_ref[...] = jnp.where(lane == 0, s1, jnp.where(lane == 1, s2, 0.0))
        run_ref[...] = run_ref[...] + col_sum


def _moe_slots(route, t_rows, tb=512):
    m = route.shape[0]
    tb = min(tb, m)
    assert t_rows & (t_rows - 1) == 0
    acc = pltpu.VMEM((8, LANE), F32)
    return pl.pallas_call(
        functools.partial(_slot_kernel, tb=tb, t_rows=t_rows),
        grid=(2, m // tb),
        in_specs=[pl.BlockSpec((tb, LANE), lambda ph, i: (i, 0))],
        out_specs=[pl.BlockSpec((tb, LANE), lambda ph, i: (i * ph, 0)),
                   pl.BlockSpec((8, LANE), lambda ph, i: (0, 0))],
        out_shape=[jax.ShapeDtypeStruct((m, LANE), F32), jax.ShapeDtypeStruct((8, LANE), F32)],
        scratch_shapes=[acc, acc, acc],
        compiler_params=_cparams(2),
        name="moe_slots",
    )(route)


def _routing_tables(route, n_exp, t_rows, tm):
    m = route.shape[0]
    slots_f, ends_f = _moe_slots(route, t_rows)
    slot2 = slots_f[:, 0:2].astype(jnp.int32)
    ends = ends_f[0, :n_exp].astype(jnp.int32)
    n_slots = 2 * m + n_exp * t_rows
    n_tiles = n_slots // t_rows
    src_rows = jnp.zeros((n_slots,), jnp.int32).at[slot2.reshape(-1)].set(jnp.arange(2 * m, dtype=jnp.int32) // 2)
    tile_start = jnp.arange(n_tiles, dtype=jnp.int32) * t_rows
    tile_valid = (tile_start < ends[-1]).astype(jnp.int32)
    last_valid = jnp.maximum(ends[-1] - 1, 0)
    tile_expert = jnp.searchsorted(ends, jnp.minimum(tile_start, last_valid), side="right").astype(jnp.int32)
    tile_expert = jnp.minimum(tile_expert, n_exp - 1)
    slots = jnp.concatenate([slot2[:, 0].reshape(m // tm, 1, tm), slot2[:, 1].reshape(m // tm, 1, tm)], axis=2)
    return src_rows, tile_expert, tile_valid, slots


def _ple_kernel(a_ref, w_ref, x_ref, p_ref, wp_ref, of_ref, ob_ref, wb_ref):
    @pl.when(pl.program_id(1) == 0)
    def _():
        wb_ref[...] = w_ref[0].astype(BF16)

    gate = _sigmoid(_dot(a_ref[...], wb_ref[...]))
    o = x_ref[...] + gate * _dot(p_ref[0].astype(BF16), wp_ref[0].astype(BF16))
    of_ref[...] = o
    ob_ref[...] = o.astype(BF16)


def _ple(xb, xf, p_all, w_gate_all, w_ple_all, layer, tm=1024, tn=512):
    m, d = xb.shape
    pd = p_all.shape[2]
    tm, tn = _tile(m, tm), _tile(d, tn)
    out = pl.BlockSpec((tm, tn), lambda c, i: (i, c))
    return pl.pallas_call(
        _ple_kernel,
        grid=(d // tn, m // tm),
        in_specs=[pl.BlockSpec((tm, d), lambda c, i: (i, 0)),
                  pl.BlockSpec((1, d, tn), lambda c, i: (layer, 0, c)),
                  out,
                  pl.BlockSpec((1, tm, pd), lambda c, i: (layer, i, 0)),
                  pl.BlockSpec((1, pd, tn), lambda c, i: (layer, 0, c))],
        out_specs=[out, out],
        out_shape=[jax.ShapeDtypeStruct((m, d), F32), jax.ShapeDtypeStruct((m, d), BF16)],
        scratch_shapes=[pltpu.VMEM((d, tn), BF16)],
        compiler_params=_cparams(2),
        name="ple",
    )(xb, w_gate_all, xf, p_all, w_ple_all)


def kernel(x, p, positions, w_in_even, conv_w, conv_b, conv_ln_g, conv_ln_b, fox_f_bias, w_out_even, w_in_odd, ret_norm_g, w_out_odd, ln_mix_g, ln_mix_b, ln_ffn_g, ln_ffn_b, w_router, b_router, w_gate, w_up, w_down, w_ple, w_ple_gate):
    batch, seq, d_model = x.shape
    depth = ln_mix_g.shape[0]
    m = batch * seq
    gw = d_model // 2
    ch = conv_w.shape[2]
    fox_heads = fox_f_bias.shape[1]
    fox_hd = gw // fox_heads
    ret_hd = gw // RET_HEADS
    sb_hd = gw // SB_HEADS
    n_exp = w_router.shape[1]
    alpha = (2 * depth) ** 0.25
    moe_rows = min(256, m)
    comb_rows = min(128, m)
    fox_blk = min(512, seq)

    xf = x.reshape(m, d_model)
    xb = xf.astype(BF16)
    p3 = p.reshape(depth, m, p.shape[-1])
    w_gate_b, w_up_b, w_down_b = w_gate.astype(BF16), w_up.astype(BF16), w_down.astype(BF16)
    w_router_pad = jnp.zeros((d_model, LANE), F32).at[:, :n_exp].set(w_router).astype(BF16)
    b_router_pad = jnp.full((1, LANE), NEG, F32).at[0, :n_exp].set(b_router)

    inv = ROPE_BASE ** (-jnp.arange(0, ret_hd, 2, dtype=F32) / ret_hd)
    log_g = jnp.log1p(-jnp.exp2(-5.0 - jnp.arange(RET_HEADS, dtype=F32)))
    cos, sin = _rope_tables(positions.reshape(m, 1).astype(F32), jnp.repeat(inv, 2).reshape(1, ret_hd))

    n_main = 2 * ch + 3 * gw
    even_scale = jnp.ones((n_main,), F32).at[2 * ch:2 * ch + gw].set(fox_hd ** -0.5)
    odd_scale = jnp.ones((7 * gw,), F32).at[:gw].set(ret_hd ** -0.5).at[4 * gw:5 * gw].set(sb_hd ** -0.5)

    for i in range(depth):
        j = i // 2
        if i % 2 == 0:
            h = _proj_in(xb, w_in_even, j, n_main, even_scale)
            w_f = jnp.zeros((d_model, LANE), F32).at[:, :fox_heads].set(w_in_even[j, :, n_main:]).astype(BF16)
            f_logit = _matmul(xb, w_f, F32)[:, :fox_heads]
            f_rows = f_logit.reshape(batch, seq, fox_heads).transpose(0, 2, 1).reshape(batch * fox_heads, seq)
            bias_rows = jnp.tile(fox_f_bias[j], batch).reshape(batch * fox_heads, 1)
            cum = _fox_cum(f_rows, bias_rows).reshape(batch * fox_heads, seq // fox_blk, 1, fox_blk)
            conv_out = _conv_module(h, batch, seq, ch, conv_w[j], conv_b[j], conv_ln_g[j], conv_ln_b[j])
            q_col = 2 * ch // fox_hd
            fox_out = _fox_attention(h, cum, batch, seq, fox_heads, fox_hd,
                                     q_col, q_col + fox_heads, q_col + 2 * fox_heads, blk=fox_blk)
            mixed = _proj_out(conv_out, fox_out, w_out_even, j)
        else:
            h = _proj_in(xb, w_in_odd, j, 7 * gw, odd_scale)
            ret_out = _retention(h, cos, sin, log_g, ret_norm_g[j], batch, seq, RET_HEADS, ret_hd)
            s_col = 4 * gw // sb_hd
            sb_out = _sb_attention(h, batch, seq, SB_HEADS, sb_hd, s_col, s_col + SB_HEADS, s_col + 2 * SB_HEADS)
            mixed = _proj_out(ret_out, sb_out, w_out_odd, j)
        xf, xb = _res_ln(xf, mixed, ln_mix_g[i], ln_mix_b[i], alpha)
        route = _router(xb, w_router_pad, b_router_pad, n_exp)
        src_rows, tile_expert, tile_valid, slots = _routing_tables(route, n_exp, moe_rows, comb_rows)
        y = _moe_ffn(xf, src_rows, tile_expert, tile_valid, w_gate_b, w_up_b, w_down_b, i, moe_rows)
        xf, xb = _moe_combine_ln(xf, y, route, slots, ln_ffn_g[i], ln_ffn_b[i], alpha, tm=comb_rows)
        xf, xb = _ple(xb, xf, p3, w_ple_gate, w_ple, i)
    return xf.reshape(batch, seq, d_model)
```

```python
import functools

import jax
import jax.numpy as jnp
from jax import lax
from jax.experimental import pallas as pl
from jax.experimental.pallas import tpu as pltpu

F32 = jnp.float32
BF16 = jnp.bfloat16
FP8 = jnp.float8_e4m3fn
FP8_PEAK = 256.0

CHUNK = 64
N_GROUPS = 4
RET_HEADS = 8
SB_HEADS = 16
ROPE_BASE = 10000.0
LN_EPS = 1e-5
NEG = -1e30

LANE = 128
SUBLANE = 8
VMEM_LIMIT_BYTES = 56 * 1024 * 1024


def _cparams(n_axes):
    return pltpu.CompilerParams(dimension_semantics=("arbitrary",) * n_axes,
                                vmem_limit_bytes=VMEM_LIMIT_BYTES)


def _sigmoid(x):
    return 1.0 / (1.0 + jnp.exp(-x))


def _dot(a, b):
    return jnp.dot(a, b, preferred_element_type=F32)


def _dot_nt(a, b):
    return lax.dot_general(a, b, (((1,), (1,)), ((), ())), preferred_element_type=F32)


def _mm_kernel(a_ref, b_ref, o_ref):
    o_ref[...] = _dot(a_ref[...], b_ref[...]).astype(o_ref.dtype)


def _tile(n, pref):
    t = min(pref, n)
    while n % t or (t % LANE and t != n):
        t -= LANE if t % LANE == 0 else t % LANE
    return t


def _matmul(a, b, out_dtype, tm=1024, tn=512):
    m, kd = a.shape
    n = b.shape[1]
    tm, tn = _tile(m, tm), _tile(n, tn)
    return pl.pallas_call(
        _mm_kernel,
        grid=(m // tm, n // tn),
        in_specs=[pl.BlockSpec((tm, kd), lambda i, j: (i, 0)),
                  pl.BlockSpec((kd, tn), lambda i, j: (0, j))],
        out_specs=pl.BlockSpec((tm, tn), lambda i, j: (i, j)),
        out_shape=jax.ShapeDtypeStruct((m, n), out_dtype),
        compiler_params=_cparams(2),
        name="matmul",
    )(a, b)


def _quant_rows_kernel(x_ref, q_ref, s_ref):
    x = x_ref[...].astype(F32)
    amax = jnp.max(jnp.abs(x), axis=-1, keepdims=True)
    scale = jnp.where(amax > 0.0, amax * (1.0 / FP8_PEAK), 1.0)
    q_ref[...] = (x / scale).astype(FP8)
    s_ref[...] = jnp.broadcast_to(scale, s_ref.shape)


def _quant_rows(x, tm=512):
    m, kd = x.shape
    tm = _tile(m, tm)
    return pl.pallas_call(
        _quant_rows_kernel,
        grid=(m // tm,),
        in_specs=[pl.BlockSpec((tm, kd), lambda i: (i, 0))],
        out_specs=[pl.BlockSpec((tm, kd), lambda i: (i, 0)), pl.BlockSpec((tm, LANE), lambda i: (i, 0))],
        out_shape=[jax.ShapeDtypeStruct((m, kd), FP8), jax.ShapeDtypeStruct((m, LANE), F32)],
        compiler_params=_cparams(1),
        name="quant_rows",
    )(x)


def _quant_cols(w_ref, col_mult, wq_ref, ws_ref):
    kd, tn = wq_ref.shape
    chunk = _tile(kd, 512)

    def amax_body(c, acc):
        w = w_ref[0, pl.ds(pl.multiple_of(c * chunk, chunk), chunk), :]
        return jnp.maximum(acc, jnp.max(jnp.abs(w), axis=0, keepdims=True))

    amax = lax.fori_loop(0, kd // chunk, amax_body, jnp.zeros((1, tn), F32)) * jnp.abs(col_mult)
    scale = jnp.where(amax > 0.0, amax * (1.0 / FP8_PEAK), 1.0)
    mult = col_mult / scale

    def quant_body(c, carry):
        rows = pl.ds(pl.multiple_of(c * chunk, chunk), chunk)
        wq_ref[rows, :] = (w_ref[0, rows, :] * mult).astype(FP8)
        return carry

    lax.fori_loop(0, kd // chunk, quant_body, 0)
    ws_ref[...] = jnp.broadcast_to(scale, ws_ref.shape)


def _proj_in_kernel(a_ref, r_ref, w_ref, s_ref, o_ref, wq_ref, ws_ref):
    @pl.when(pl.program_id(1) == 0)
    def _():
        _quant_cols(w_ref, s_ref[...], wq_ref, ws_ref)

    o_ref[...] = (_dot(a_ref[...], wq_ref[...]) * r_ref[:, 0:1] * ws_ref[0:1, :]).astype(o_ref.dtype)


def _proj_in(a8, a_scale, w_all, layer, n_cols, col_scale, tm=1024, tn=1024):
    m, kd = a8.shape
    tm, tn = _tile(m, tm), _tile(n_cols, tn)
    return pl.pallas_call(
        _proj_in_kernel,
        grid=(n_cols // tn, m // tm),
        in_specs=[pl.BlockSpec((tm, kd), lambda n, i: (i, 0)),
                  pl.BlockSpec((tm, LANE), lambda n, i: (i, 0)),
                  pl.BlockSpec((1, kd, tn), lambda n, i: (layer, 0, n)),
                  pl.BlockSpec((1, tn), lambda n, i: (0, n))],
        out_specs=pl.BlockSpec((tm, tn), lambda n, i: (i, n)),
        out_shape=jax.ShapeDtypeStruct((m, n_cols), BF16),
        scratch_shapes=[pltpu.VMEM((kd, tn), FP8), pltpu.VMEM((SUBLANE, tn), F32)],
        compiler_params=_cparams(2),
        name="proj_in",
    )(a8, a_scale, w_all, col_scale.reshape(1, n_cols))


def _proj_out_kernel(a1_ref, r1_ref, a2_ref, r2_ref, w1_ref, w2_ref, o_ref, wq1_ref, ws1_ref, wq2_ref, ws2_ref):
    @pl.when(pl.program_id(1) == 0)
    def _():
        _quant_cols(w1_ref, jnp.ones((1, wq1_ref.shape[1]), F32), wq1_ref, ws1_ref)
        _quant_cols(w2_ref, jnp.ones((1, wq2_ref.shape[1]), F32), wq2_ref, ws2_ref)

    o_ref[...] = (_dot(a1_ref[...], wq1_ref[...]) * r1_ref[:, 0:1] * ws1_ref[0:1, :]
                  + _dot(a2_ref[...], wq2_ref[...]) * r2_ref[:, 0:1] * ws2_ref[0:1, :])


def _proj_out(a1, r1, a2, r2, w_all, layer, tm=1024, tn=512):
    m, k1 = a1.shape
    n = w_all.shape[2]
    assert a2.shape[1] == k1 and w_all.shape[1] == 2 * k1
    tm, tn = _tile(m, tm), _tile(n, tn)
    act = pl.BlockSpec((tm, k1), lambda c, i: (i, 0))
    rsc = pl.BlockSpec((tm, LANE), lambda c, i: (i, 0))
    return pl.pallas_call(
        _proj_out_kernel,
        grid=(n // tn, m // tm),
        in_specs=[act, rsc, act, rsc,
                  pl.BlockSpec((1, k1, tn), lambda c, i: (layer, 0, c)),
                  pl.BlockSpec((1, k1, tn), lambda c, i: (layer, 1, c))],
        out_specs=pl.BlockSpec((tm, tn), lambda c, i: (i, c)),
        out_shape=jax.ShapeDtypeStruct((m, n), F32),
        scratch_shapes=[pltpu.VMEM((k1, tn), FP8), pltpu.VMEM((SUBLANE, tn), F32),
                        pltpu.VMEM((k1, tn), FP8), pltpu.VMEM((SUBLANE, tn), F32)],
        compiler_params=_cparams(2),
        name="proj_out",
    )(a1, r1, a2, r2, w_all, w_all)


def _layernorm_rows(t, g, b):
    mu = jnp.mean(t, axis=-1, keepdims=True)
    d = t - mu
    var = jnp.mean(d * d, axis=-1, keepdims=True)
    return d * lax.rsqrt(var + LN_EPS) * g + b


def _res_ln_kernel(x_ref, y_ref, g_ref, b_ref, of_ref, ob_ref, *, alpha):
    o = _layernorm_rows(alpha * x_ref[...] + y_ref[...], g_ref[...], b_ref[...])
    of_ref[...] = o
    ob_ref[...] = o.astype(BF16)


def _res_ln(x, y, g, b, alpha, tm=256):
    m, d = x.shape
    tm = min(tm, m)
    row = pl.BlockSpec((tm, d), lambda i: (i, 0))
    vec = pl.BlockSpec((1, d), lambda i: (0, 0))
    return pl.pallas_call(
        functools.partial(_res_ln_kernel, alpha=alpha),
        grid=(m // tm,),
        in_specs=[row, row, vec, vec],
        out_specs=[row, row],
        out_shape=[jax.ShapeDtypeStruct((m, d), F32), jax.ShapeDtypeStruct((m, d), BF16)],
        compiler_params=_cparams(1),
        name="res_ln",
    )(x, y, g.reshape(1, d), b.reshape(1, d))


def _conv_kernel(a_ref, g_ref, ap_ref, gp_ref, cw_ref, cb_ref, lg_ref, lb_ref, o_ref, buf_ref, y_ref,
                 *, t_rows, halo, width):
    i = pl.program_id(1)
    ch = a_ref.shape[1]
    buf_ref[pl.ds(halo, t_rows), :] = a_ref[...].astype(F32) * _sigmoid(g_ref[...].astype(F32))
    ap = ap_ref[pl.ds(t_rows - halo, halo), :].astype(F32)
    gp = gp_ref[pl.ds(t_rows - halo, halo), :].astype(F32)
    tail = ap * _sigmoid(gp)
    buf_ref[pl.ds(0, halo), :] = jnp.where(i > 0, tail, 0.0)
    base = halo - (width - 1)
    for c0 in range(0, ch, LANE):
        acc = jnp.zeros((t_rows, LANE), F32) + cb_ref[:, c0:c0 + LANE]
        for r in range(SUBLANE):
            taps = [w for w in range(width) if (base + w) % SUBLANE == r]
            if not taps:
                continue
            rows = t_rows + (SUBLANE if r else 0)
            z = None
            for w in taps:
                term = buf_ref[pl.ds(base + w - r, rows), c0:c0 + LANE] * cw_ref[w:w + 1, c0:c0 + LANE]
                z = term if z is None else z + term
            acc = acc + z[r:r + t_rows, :]
        y_ref[:, c0:c0 + LANE] = acc
    y = _layernorm_rows(y_ref[...], lg_ref[...], lb_ref[...])
    o_ref[...] = (y * _sigmoid(y)).astype(o_ref.dtype)


def _conv_module(h, batch, seq, ch, conv_w, conv_b, ln_g, ln_b, t_rows=256, halo=32):
    width = conv_w.shape[0]
    t_rows = min(t_rows, seq)
    assert width - 1 <= halo <= t_rows and seq % t_rows == 0 and halo % SUBLANE == 0
    nt = seq // t_rows
    cur = lambda col: pl.BlockSpec((t_rows, ch), lambda b, i: (b * nt + i, col))
    prev = lambda col: pl.BlockSpec((t_rows, ch), lambda b, i: (b * nt + jnp.maximum(i - 1, 0), col))
    vec = pl.BlockSpec((1, ch), lambda b, i: (0, 0))
    return pl.pallas_call(
        functools.partial(_conv_kernel, t_rows=t_rows, halo=halo, width=width),
        grid=(batch, nt),
        in_specs=[cur(0), cur(1), prev(0), prev(1),
                  pl.BlockSpec((width, ch), lambda b, i: (0, 0)), vec, vec, vec],
        out_specs=pl.BlockSpec((t_rows, ch), lambda b, i: (b * nt + i, 0)),
        out_shape=jax.ShapeDtypeStruct((batch * seq, ch), BF16),
        scratch_shapes=[pltpu.VMEM((t_rows + halo, ch), F32), pltpu.VMEM((t_rows, ch), F32)],
        compiler_params=_cparams(2),
        name="conv_module",
    )(h, h, h, h, conv_w, conv_b.reshape(1, ch), ln_g.reshape(1, ch), ln_b.reshape(1, ch))


def _cum_kernel(f_ref, b_ref, o_ref):
    x = f_ref[...] + b_ref[...]
    ls = jnp.minimum(x, 0.0) - jnp.log(1.0 + jnp.exp(-jnp.abs(x)))
    lane = lax.broadcasted_iota(jnp.int32, ls.shape, 1)
    sh = 1
    while sh < ls.shape[1]:
        ls = ls + jnp.where(lane >= sh, pltpu.roll(ls, sh, 1), 0.0)
        sh *= 2
    o_ref[...] = ls


def _fox_cum(f_rows, bias_rows):
    r, s = f_rows.shape
    return pl.pallas_call(
        _cum_kernel,
        grid=(1,),
        in_specs=[pl.BlockSpec((r, s), lambda i: (0, 0)), pl.BlockSpec((r, 1), lambda i: (0, 0))],
        out_specs=pl.BlockSpec((r, s), lambda i: (0, 0)),
        out_shape=jax.ShapeDtypeStruct((r, s), F32),
        compiler_params=_cparams(1),
        name="fox_cum",
    )(f_rows, bias_rows)


def _fox_kernel(q_ref, k_ref, v_ref, c_ref, o_ref, *, blk):
    i = pl.program_id(2)
    hd = q_ref.shape[1]
    q = q_ref[...]
    row = lax.broadcasted_iota(jnp.int32, (blk, blk), 0)
    col = lax.broadcasted_iota(jnp.int32, (blk, blk), 1)

    def step(kj, carry, masked):
        m, l, acc = carry
        start = pl.multiple_of(kj * blk, blk)
        k = k_ref[pl.ds(start, blk), :]
        v = v_ref[pl.ds(start, blk), :]
        s = _dot_nt(q, k) - c_ref[0, kj]
        if masked:
            s = jnp.where(col <= row, s, NEG)
        m_new = jnp.maximum(m, jnp.max(s, axis=-1, keepdims=True))
        p = jnp.exp(s - m_new)
        alpha = jnp.exp(m - m_new)
        l = alpha * l + jnp.sum(p, axis=-1, keepdims=True)
        acc = alpha * acc + _dot(p.astype(BF16), v)
        return m_new, l, acc

    init = (jnp.full((blk, 1), NEG, F32), jnp.zeros((blk, 1), F32), jnp.zeros((blk, hd), F32))
    carry = lax.fori_loop(0, i, lambda kj, c: step(kj, c, False), init)
    _, l, acc = step(i, carry, True)
    o_ref[...] = (acc / l).astype(o_ref.dtype)


def _fox_attention(h, cum, batch, seq, n_heads, hd, q_col, k_col, v_col, blk=512):
    blk = min(blk, seq)
    nq = seq // blk
    return pl.pallas_call(
        functools.partial(_fox_kernel, blk=blk),
        grid=(batch, n_heads, nq),
        in_specs=[pl.BlockSpec((blk, hd), lambda b, hh, i: (b * nq + i, q_col + hh)),
                  pl.BlockSpec((seq, hd), lambda b, hh, i: (b, k_col + hh)),
                  pl.BlockSpec((seq, hd), lambda b, hh, i: (b, v_col + hh)),
                  pl.BlockSpec((1, nq, 1, blk), lambda b, hh, i: (b * n_heads + hh, 0, 0, 0))],
        out_specs=pl.BlockSpec((blk, hd), lambda b, hh, i: (b * nq + i, hh)),
        out_shape=jax.ShapeDtypeStruct((batch * seq, n_heads * hd), BF16),
        compiler_params=_cparams(3),
        name="fox_attention",
    )(h, h, h, cum)


SB_DEAD = 105.0


def _sb_kernel(q_ref, k_ref, v_ref, o_ref, *, blk, heads, hd):
    i = pl.program_id(2)
    jr = lax.broadcasted_iota(jnp.int32, (blk, blk), 0)
    jc = lax.broadcasted_iota(jnp.int32, (blk, blk), 1)
    upper = jnp.where(jr > jc, 1.0, 0.0).astype(BF16)
    strict = jc < jr

    def block(hh, start, run, acc, masked):
        cols = slice(hh * hd, (hh + 1) * hd)
        z = _dot_nt(q_ref[:, cols], k_ref[pl.ds(start, blk), cols])
        sp = jnp.maximum(z, 0.0) + jnp.log(1.0 + jnp.exp(-jnp.abs(z)))
        log_1m = -sp
        log_b = z - sp
        if masked:
            log_1m = jnp.where(strict, log_1m, 0.0)
        hi = log_1m.astype(BF16)
        lo = (log_1m - hi.astype(F32)).astype(BF16)
        later = _dot(hi, upper) + _dot(lo, upper)
        w = jnp.exp(log_b + later + run)
        if masked:
            w = jnp.where(strict, w, 0.0)
        acc = acc + _dot(w.astype(BF16), v_ref[pl.ds(start, blk), cols])
        return run + later[:, 0:1] + log_1m[:, 0:1], acc

    def step(kj, carry, masked):
        start = pl.multiple_of(kj * blk, blk)
        return tuple(block(hh, start, *carry[hh], masked) for hh in range(heads))

    def live(carry):
        return functools.reduce(jnp.maximum, [jnp.max(run) for run, _ in carry])

    carry = tuple((jnp.zeros((blk, 1), F32), jnp.zeros((blk, hd), F32)) for _ in range(heads))
    carry = step(i, carry, True)

    def cond(state):
        t, top, _ = state
        return jnp.logical_and(t < i, top > -SB_DEAD)

    def body(state):
        t, _, c = state
        c = step(i - 1 - t, c, False)
        return t + 1, live(c), c

    _, _, carry = lax.while_loop(cond, body, (jnp.int32(0), live(carry), carry))
    for hh in range(heads):
        o_ref[:, hh * hd:(hh + 1) * hd] = carry[hh][1].astype(o_ref.dtype)


def _sb_attention(h, batch, seq, n_heads, hd, q_col, k_col, v_col, blk=256, heads=2):
    blk = min(blk, seq)
    nq = seq // blk
    assert n_heads % heads == 0 and q_col % heads == 0 and k_col % heads == 0 and v_col % heads == 0
    wide = heads * hd
    return pl.pallas_call(
        functools.partial(_sb_kernel, blk=blk, heads=heads, hd=hd),
        grid=(batch, n_heads // heads, nq),
        in_specs=[pl.BlockSpec((blk, wide), lambda b, hh, i: (b * nq + i, q_col // heads + hh)),
                  pl.BlockSpec((seq, wide), lambda b, hh, i: (b, k_col // heads + hh)),
                  pl.BlockSpec((seq, wide), lambda b, hh, i: (b, v_col // heads + hh))],
        out_specs=pl.BlockSpec((blk, wide), lambda b, hh, i: (b * nq + i, hh)),
        out_shape=jax.ShapeDtypeStruct((batch * seq, n_heads * hd), BF16),
        compiler_params=_cparams(3),
        name="sb_attention",
    )(h, h, h)


def _rope_kernel(pos_ref, inv_ref, cos_ref, sin_ref):
    ang = pos_ref[...] * inv_ref[...]
    lane = lax.broadcasted_iota(jnp.int32, ang.shape, 1)
    cos_ref[...] = jnp.cos(ang)
    sin_ref[...] = jnp.where((lane & 1) == 0, -jnp.sin(ang), jnp.sin(ang))


def _rope_tables(pos_col, inv_row, tm=512):
    m = pos_col.shape[0]
    half = inv_row.shape[1]
    tm = min(tm, m)
    out = pl.BlockSpec((tm, half), lambda i: (i, 0))
    return pl.pallas_call(
        _rope_kernel,
        grid=(m // tm,),
        in_specs=[pl.BlockSpec((tm, 1), lambda i: (i, 0)), pl.BlockSpec((1, half), lambda i: (0, 0))],
        out_specs=[out, out],
        out_shape=[jax.ShapeDtypeStruct((m, half), F32)] * 2,
        compiler_params=_cparams(1),
        name="rope_tables",
    )(pos_col, inv_row)


def _ret_kernel(lg_ref, q_ref, k_ref, v_ref, gate_ref, cos_ref, sin_ref, gn_ref, o_ref, state_ref, *, blk):
    hh = pl.program_id(1)
    i = pl.program_id(2)
    d = q_ref.shape[1]
    lg = lg_ref[hh]

    @pl.when(i == 0)
    def _():
        state_ref[...] = jnp.zeros_like(state_ref)

    cos = cos_ref[...]
    sin = sin_ref[...]
    even = (lax.broadcasted_iota(jnp.int32, (blk, d), 1) & 1) == 0

    def rot(t_ref):
        t = t_ref[...].astype(F32)
        partner = jnp.where(even, pltpu.roll(t, d - 1, 1), pltpu.roll(t, 1, 1))
        return t * cos + partner * sin

    qr = rot(q_ref)
    kr = rot(k_ref)
    v = v_ref[...]
    n = lax.broadcasted_iota(jnp.int32, (blk, 1), 0).astype(F32)
    q_decay = jnp.exp((n + 1.0) * lg)
    k_decay = jnp.exp((blk - 1.0 - n) * lg)
    row = lax.broadcasted_iota(jnp.int32, (blk, blk), 0)
    col = lax.broadcasted_iota(jnp.int32, (blk, blk), 1)
    dist = jnp.abs(row - col).astype(F32)
    shift = CHUNK.bit_length() - 1
    seen = (col >> shift) <= (row >> shift)
    decay = jnp.where(seen, jnp.exp(dist * lg), 0.0)
    scores = _dot_nt(qr.astype(BF16), kr.astype(BF16)) * decay
    out = _dot(scores.astype(BF16), v) + _dot((qr * q_decay).astype(BF16), state_ref[...].astype(BF16))
    kv = lax.dot_general((kr * k_decay).astype(BF16), v, (((0,), (0,)), ((), ())), preferred_element_type=F32)
    state_ref[...] = jnp.exp(jnp.full((1, d), blk * lg, F32)) * state_ref[...] + kv
    mu = jnp.mean(out, axis=-1, keepdims=True)
    dv = out - mu
    var = jnp.mean(dv * dv, axis=-1, keepdims=True)
    y = dv * lax.rsqrt(var + LN_EPS) * gn_ref[...]
    g = gate_ref[...].astype(F32)
    o_ref[...] = (y * (g * _sigmoid(g))).astype(o_ref.dtype)


def _retention(h, cos, sin, log_g, gn, batch, seq, n_heads, d, blk=256):
    blk = min(blk, seq)
    assert blk % CHUNK == 0 and CHUNK & (CHUNK - 1) == 0
    nb = seq // blk
    hcol = lambda off: pl.BlockSpec((blk, d), lambda b, hh, i: (b * nb + i, off + hh))
    tab = pl.BlockSpec((blk, d), lambda b, hh, i: (b * nb + i, 0))
    return pl.pallas_call(
        functools.partial(_ret_kernel, blk=blk),
        grid=(batch, n_heads, nb),
        in_specs=[pl.BlockSpec(memory_space=pltpu.SMEM),
                  hcol(0), hcol(n_heads), hcol(2 * n_heads), hcol(3 * n_heads), tab, tab,
                  pl.BlockSpec((1, d), lambda b, hh, i: (0, hh))],
        out_specs=pl.BlockSpec((blk, d), lambda b, hh, i: (b * nb + i, hh)),
        out_shape=jax.ShapeDtypeStruct((batch * seq, n_heads * d), BF16),
        scratch_shapes=[pltpu.VMEM((d, d), F32)],
        compiler_params=_cparams(3),
        name="retention",
    )(log_g, h, h, h, h, cos, sin, gn.reshape(1, n_heads * d))


def _router_kernel(x_ref, w_ref, b_ref, o_ref, *, n_exp, epg):
    logits = _dot(x_ref[...], w_ref[...]) + b_ref[...]
    lane = lax.broadcasted_iota(jnp.int32, logits.shape, 1)
    lmax = jnp.max(logits, axis=-1, keepdims=True)
    p = jnp.where(lane < n_exp, jnp.exp(logits - lmax), -1.0)
    p1 = jnp.max(p, axis=-1, keepdims=True)
    e1 = jnp.min(jnp.where(p == p1, lane, LANE), axis=-1, keepdims=True)
    shift = epg.bit_length() - 1
    in_group = (lane >> shift) == (e1 >> shift)
    cand = jnp.where(in_group, jnp.where(lane == e1, -1.0, p), -1.0)
    p2 = jnp.max(cand, axis=-1, keepdims=True)
    e2 = jnp.min(jnp.where(cand == p2, lane, LANE), axis=-1, keepdims=True)
    tot = p1 + p2
    o_ref[...] = jnp.where(lane == 0, e1.astype(F32),
                           jnp.where(lane == 1, e2.astype(F32),
                                     jnp.where(lane == 2, p1 / tot, jnp.where(lane == 3, p2 / tot, 0.0))))


def _router(xb, w_pad, b_pad, n_exp, tm=512):
    m, d = xb.shape
    tm = min(tm, m)
    epg = n_exp // N_GROUPS
    assert epg & (epg - 1) == 0 and n_exp <= LANE
    return pl.pallas_call(
        functools.partial(_router_kernel, n_exp=n_exp, epg=epg),
        grid=(m // tm,),
        in_specs=[pl.BlockSpec((tm, d), lambda i: (i, 0)),
                  pl.BlockSpec((d, LANE), lambda i: (0, 0)),
                  pl.BlockSpec((1, LANE), lambda i: (0, 0))],
        out_specs=pl.BlockSpec((tm, LANE), lambda i: (i, 0)),
        out_shape=jax.ShapeDtypeStruct((m, LANE), F32),
        compiler_params=_cparams(1),
        name="router",
    )(xb, w_pad, b_pad)


GATHER_UNROLL = 8


def _row_copy(src_hbm, row, dst_ref, r, sem):
    return pltpu.make_async_copy(src_hbm.at[pl.ds(row, 1), :], dst_ref.at[pl.ds(r, 1), :], sem)


def _gather_start(idx_ref, n_rows, src_hbm, dst_ref, sem):
    def issue(r, carry):
        _row_copy(src_hbm, idx_ref[0, 0, r], dst_ref, r, sem).start()
        return carry
    lax.fori_loop(0, n_rows, issue, 0, unroll=GATHER_UNROLL)


def _gather_wait(n_rows, src_hbm, dst_ref, sem):
    def drain(r, carry):
        _row_copy(src_hbm, 0, dst_ref, r, sem).wait()
        return carry
    lax.fori_loop(0, n_rows, drain, 0, unroll=GATHER_UNROLL)


def _moe_kernel(te_ref, tv_ref, cur_ref, nxt_ref, x_hbm, wg_ref, wu_ref, wd_ref, y_ref, xbuf, sem, *, t_rows):
    t = pl.program_id(0)
    slot = t % 2

    @pl.when(t == 0)
    def _():
        _gather_start(cur_ref, t_rows, x_hbm, xbuf.at[0], sem.at[0])

    @pl.when(jnp.logical_and(t + 1 < pl.num_programs(0), tv_ref[jnp.minimum(t + 1, pl.num_programs(0) - 1)] == 1))
    def _():
        _gather_start(nxt_ref, t_rows, x_hbm, xbuf.at[1 - slot], sem.at[1 - slot])

    @pl.when(tv_ref[t] == 1)
    def _():
        _gather_wait(t_rows, x_hbm, xbuf.at[slot], sem.at[slot])
        xb = xbuf[slot].astype(BF16)
        hg = _dot(xb, wg_ref[0, 0])
        hu = _dot(xb, wu_ref[0, 0])
        hidden = (hg * _sigmoid(hg) * hu).astype(BF16)
        y_ref[...] = _dot(hidden, wd_ref[0, 0])

    @pl.when(tv_ref[t] == 0)
    def _():
        y_ref[...] = jnp.zeros_like(y_ref)


def _moe_ffn(x, src_rows, tile_expert, tile_valid, w_gate, w_up, w_down, layer, t_rows):
    n_tiles = tile_expert.shape[0]
    d = x.shape[1]
    f = w_gate.shape[3]
    wspec = lambda a, b: pl.BlockSpec((1, 1, a, b), lambda t, te, tv: (layer, te[t], 0, 0))
    grid_spec = pltpu.PrefetchScalarGridSpec(
        num_scalar_prefetch=2,
        grid=(n_tiles,),
        in_specs=[pl.BlockSpec((1, 1, t_rows), lambda t, te, tv: (t, 0, 0), memory_space=pltpu.SMEM),
                  pl.BlockSpec((1, 1, t_rows), lambda t, te, tv: (jnp.minimum(t + 1, n_tiles - 1), 0, 0),
                               memory_space=pltpu.SMEM),
                  pl.BlockSpec(memory_space=pl.ANY),
                  wspec(d, f), wspec(d, f), wspec(f, d)],
        out_specs=pl.BlockSpec((t_rows, d), lambda t, te, tv: (t, 0)),
        scratch_shapes=[pltpu.VMEM((2, t_rows, d), F32), pltpu.SemaphoreType.DMA((2,))],
    )
    src3 = src_rows.reshape(n_tiles, 1, t_rows)
    return pl.pallas_call(
        functools.partial(_moe_kernel, t_rows=t_rows),
        grid_spec=grid_spec,
        out_shape=jax.ShapeDtypeStruct((n_tiles * t_rows, d), F32),
        compiler_params=_cparams(1),
        name="moe_ffn",
    )(tile_expert, tile_valid, src3, src3, x, w_gate, w_up, w_down)


def _combine_kernel(cur_ref, nxt_ref, route_ref, x_ref, y_hbm, g_ref, b_ref, of_ref, ob_ref, buf, sem, *, tm, alpha):
    i = pl.program_id(0)
    slot = i % 2

    @pl.when(i == 0)
    def _():
        _gather_start(cur_ref, 2 * tm, y_hbm, buf.at[0], sem.at[0])

    @pl.when(i + 1 < pl.num_programs(0))
    def _():
        _gather_start(nxt_ref, 2 * tm, y_hbm, buf.at[1 - slot], sem.at[1 - slot])

    _gather_wait(2 * tm, y_hbm, buf.at[slot], sem.at[slot])
    w1 = route_ref[:, 2:3]
    w2 = route_ref[:, 3:4]
    ffn = w1 * buf[slot, pl.ds(0, tm), :] + w2 * buf[slot, pl.ds(tm, tm), :]
    o = _layernorm_rows(alpha * x_ref[...] + ffn, g_ref[...], b_ref[...])
    of_ref[...] = o
    ob_ref[...] = o.astype(BF16)


def _moe_combine_ln(x, y, route, slots, g, b, alpha, tm=128):
    m, d = x.shape
    row = pl.BlockSpec((tm, d), lambda i: (i, 0))
    vec = pl.BlockSpec((1, d), lambda i: (0, 0))
    return pl.pallas_call(
        functools.partial(_combine_kernel, tm=tm, alpha=alpha),
        grid=(m // tm,),
        in_specs=[pl.BlockSpec((1, 1, 2 * tm), lambda i: (i, 0, 0), memory_space=pltpu.SMEM),
                  pl.BlockSpec((1, 1, 2 * tm), lambda i: (jnp.minimum(i + 1, m // tm - 1), 0, 0),
                               memory_space=pltpu.SMEM),
                  pl.BlockSpec((tm, LANE), lambda i: (i, 0)),
                  row,
                  pl.BlockSpec(memory_space=pl.ANY),
                  vec, vec],
        out_specs=[row, row],
        out_shape=[jax.ShapeDtypeStruct((m, d), F32), jax.ShapeDtypeStruct((m, d), BF16)],
        scratch_shapes=[pltpu.VMEM((2, 2 * tm, d), F32), pltpu.SemaphoreType.DMA((2,))],
        compiler_params=_cparams(1),
        name="moe_combine_ln",
    )(slots, slots, route, x, y, g.reshape(1, d), b.reshape(1, d))


def _slot_kernel(route_ref, slot_ref, ends_ref, cnt_ref, run_ref, start_ref, *, tb, t_rows):
    ph = pl.program_id(0)
    i = pl.program_id(1)
    r = route_ref[...]
    lane = lax.broadcasted_iota(jnp.int32, (tb, LANE), 1)
    oh1 = lane == r[:, 0:1].astype(jnp.int32)
    oh2 = lane == r[:, 1:2].astype(jnp.int32)
    both = jnp.where(oh1, 1.0, 0.0) + jnp.where(oh2, 1.0, 0.0)
    col_sum = jnp.sum(both, axis=0, keepdims=True)

    @pl.when(jnp.logical_and(ph == 0, i == 0))
    def _():
        cnt_ref[...] = jnp.zeros_like(cnt_ref)

    @pl.when(ph == 0)
    def _():
        cnt_ref[...] = cnt_ref[...] + col_sum

    @pl.when(jnp.logical_and(ph == 1, i == 0))
    def _():
        shift = t_rows.bit_length() - 1
        padded = ((cnt_ref[...].astype(jnp.int32) + (t_rows - 1)) >> shift) << shift
        lane8 = lax.broadcasted_iota(jnp.int32, padded.shape, 1)
        ends = padded
        sh = 1
        while sh < LANE:
            ends = ends + jnp.where(lane8 >= sh, pltpu.roll(ends, sh, 1), 0)
            sh *= 2
        start_ref[...] = (ends - padded).astype(F32)
        ends_ref[...] = ends.astype(F32)
        run_ref[...] = jnp.zeros_like(run_ref)

    @pl.when(ph == 1)
    def _():
        tr = lax.broadcasted_iota(jnp.int32, (tb, tb), 0)
        tc = lax.broadcasted_iota(jnp.int32, (tb, tb), 1)
        earlier = jnp.where(tr > tc, 1.0, 0.0).astype(BF16)
        before = _dot(earlier, both.astype(BF16)) + run_ref[0:1, :] + start_ref[0:1, :]
        s1 = jnp.sum(jnp.where(oh1, before, 0.0), axis=-1, keepdims=True)
        s2 = jnp.sum(jnp.where(oh2, before, 0.0), axis=-1, keepdims=True)
        slot_ref[...] = jnp.where(lane == 0, s1, jnp.where(lane == 1, s2, 0.0))
        run_ref[...] = run_ref[...] + col_sum


def _moe_slots(route, t_rows, tb=512):
    m = route.shape[0]
    tb = min(tb, m)
    assert t_rows & (t_rows - 1) == 0
    acc = pltpu.VMEM((8, LANE), F32)
    return pl.pallas_call(
        functools.partial(_slot_kernel, tb=tb, t_rows=t_rows),
        grid=(2, m // tb),
        in_specs=[pl.BlockSpec((tb, LANE), lambda ph, i: (i, 0))],
        out_specs=[pl.BlockSpec((tb, LANE), lambda ph, i: (i * ph, 0)),
                   pl.BlockSpec((8, LANE), lambda ph, i: (0, 0))],
        out_shape=[jax.ShapeDtypeStruct((m, LANE), F32), jax.ShapeDtypeStruct((8, LANE), F32)],
        scratch_shapes=[acc, acc, acc],
        compiler_params=_cparams(2),
        name="moe_slots",
    )(route)


def _routing_tables(route, n_exp, t_rows, tm):
    m = route.shape[0]
    slots_f, ends_f = _moe_slots(route, t_rows)
    slot2 = slots_f[:, 0:2].astype(jnp.int32)
    ends = ends_f[0, :n_exp].astype(jnp.int32)
    n_slots = 2 * m + n_exp * t_rows
    n_tiles = n_slots // t_rows
    src_rows = jnp.zeros((n_slots,), jnp.int32).at[slot2.reshape(-1)].set(jnp.arange(2 * m, dtype=jnp.int32) // 2)
    tile_start = jnp.arange(n_tiles, dtype=jnp.int32) * t_rows
    tile_valid = (tile_start < ends[-1]).astype(jnp.int32)
    last_valid = jnp.maximum(ends[-1] - 1, 0)
    probe = jnp.minimum(tile_start, last_valid)
    tile_expert = jnp.sum((ends[None, :] <= probe[:, None]).astype(jnp.int32), axis=1)
    tile_expert = jnp.minimum(tile_expert, n_exp - 1)
    slots = jnp.concatenate([slot2[:, 0].reshape(m // tm, 1, tm), slot2[:, 1].reshape(m // tm, 1, tm)], axis=2)
    return src_rows, tile_expert, tile_valid, slots


def _ple_kernel(a_ref, w_ref, x_ref, p_ref, wp_ref, of_ref, ob_ref, wb_ref):
    @pl.when(pl.program_id(1) == 0)
    def _():
        wb_ref[...] = w_ref[0].astype(BF16)

    gate = _sigmoid(_dot(a_ref[...], wb_ref[...]))
    o = x_ref[...] + gate * _dot(p_ref[0].astype(BF16), wp_ref[0].astype(BF16))
    of_ref[...] = o
    ob_ref[...] = o.astype(BF16)


def _ple(xb, xf, p_all, w_gate_all, w_ple_all, layer, tm=1024, tn=512):
    m, d = xb.shape
    pd = p_all.shape[2]
    tm, tn = _tile(m, tm), _tile(d, tn)
    out = pl.BlockSpec((tm, tn), lambda c, i: (i, c))
    return pl.pallas_call(
        _ple_kernel,
        grid=(d // tn, m // tm),
        in_specs=[pl.BlockSpec((tm, d), lambda c, i: (i, 0)),
                  pl.BlockSpec((1, d, tn), lambda c, i: (layer, 0, c)),
                  out,
                  pl.BlockSpec((1, tm, pd), lambda c, i: (layer, i, 0)),
                  pl.BlockSpec((1, pd, tn), lambda c, i: (layer, 0, c))],
        out_specs=[out, out],
        out_shape=[jax.ShapeDtypeStruct((m, d), F32), jax.ShapeDtypeStruct((m, d), BF16)],
        scratch_shapes=[pltpu.VMEM((d, tn), BF16)],
        compiler_params=_cparams(2),
        name="ple",
    )(xb, w_gate_all, xf, p_all, w_ple_all)


def kernel(x, p, positions, w_in_even, conv_w, conv_b, conv_ln_g, conv_ln_b, fox_f_bias, w_out_even, w_in_odd, ret_norm_g, w_out_odd, ln_mix_g, ln_mix_b, ln_ffn_g, ln_ffn_b, w_router, b_router, w_gate, w_up, w_down, w_ple, w_ple_gate):
    batch, seq, d_model = x.shape
    depth = ln_mix_g.shape[0]
    m = batch * seq
    gw = d_model // 2
    ch = conv_w.shape[2]
    fox_heads = fox_f_bias.shape[1]
    fox_hd = gw // fox_heads
    ret_hd = gw // RET_HEADS
    sb_hd = gw // SB_HEADS
    n_exp = w_router.shape[1]
    alpha = (2 * depth) ** 0.25
    moe_rows = min(256, m)
    comb_rows = min(128, m)
    fox_blk = min(512, seq)

    xf = x.reshape(m, d_model)
    xb = xf.astype(BF16)
    p3 = p.reshape(depth, m, p.shape[-1])
    w_gate_b, w_up_b, w_down_b = w_gate.astype(BF16), w_up.astype(BF16), w_down.astype(BF16)
    w_router_pad = jnp.zeros((d_model, LANE), F32).at[:, :n_exp].set(w_router).astype(BF16)
    b_router_pad = jnp.full((1, LANE), NEG, F32).at[0, :n_exp].set(b_router)

    inv = ROPE_BASE ** (-jnp.arange(0, ret_hd, 2, dtype=F32) / ret_hd)
    log_g = jnp.log1p(-jnp.exp2(-5.0 - jnp.arange(RET_HEADS, dtype=F32)))
    cos, sin = _rope_tables(positions.reshape(m, 1).astype(F32), jnp.repeat(inv, 2).reshape(1, ret_hd))

    n_main = 2 * ch + 3 * gw
    even_scale = jnp.ones((n_main,), F32).at[2 * ch:2 * ch + gw].set(fox_hd ** -0.5)
    odd_scale = jnp.ones((7 * gw,), F32).at[:gw].set(ret_hd ** -0.5).at[4 * gw:5 * gw].set(sb_hd ** -0.5)

    for i in range(depth):
        j = i // 2
        if i % 2 == 0:
            h = _proj_in(*_quant_rows(xb), w_in_even, j, n_main, even_scale)
            w_f = jnp.zeros((d_model, LANE), F32).at[:, :fox_heads].set(w_in_even[j, :, n_main:]).astype(BF16)
            f_logit = _matmul(xb, w_f, F32)[:, :fox_heads]
            f_rows = f_logit.reshape(batch, seq, fox_heads).transpose(0, 2, 1).reshape(batch * fox_heads, seq)
            bias_rows = jnp.tile(fox_f_bias[j], batch).reshape(batch * fox_heads, 1)
            cum = _fox_cum(f_rows, bias_rows).reshape(batch * fox_heads, seq // fox_blk, 1, fox_blk)
            conv_out = _conv_module(h, batch, seq, ch, conv_w[j], conv_b[j], conv_ln_g[j], conv_ln_b[j])
            q_col = 2 * ch // fox_hd
            fox_out = _fox_attention(h, cum, batch, seq, fox_heads, fox_hd,
                                     q_col, q_col + fox_heads, q_col + 2 * fox_heads, blk=fox_blk)
            mixed = _proj_out(*_quant_rows(conv_out), *_quant_rows(fox_out), w_out_even, j)
        else:
            h = _proj_in(*_quant_rows(xb), w_in_odd, j, 7 * gw, odd_scale)
            ret_out = _retention(h, cos, sin, log_g, ret_norm_g[j], batch, seq, RET_HEADS, ret_hd)
            s_col = 4 * gw // sb_hd
            sb_out = _sb_attention(h, batch, seq, SB_HEADS, sb_hd, s_col, s_col + SB_HEADS, s_col + 2 * SB_HEADS)
            mixed = _proj_out(*_quant_rows(ret_out), *_quant_rows(sb_out), w_out_odd, j)
        xf, xb = _res_ln(xf, mixed, ln_mix_g[i], ln_mix_b[i], alpha)
        route = _router(xb, w_router_pad, b_router_pad, n_exp)
        src_rows, tile_expert, tile_valid, slots = _routing_tables(route, n_exp, moe_rows, comb_rows)
        y = _moe_ffn(xf, src_rows, tile_expert, tile_valid, w_gate_b, w_up_b, w_down_b, i, moe_rows)
        xf, xb = _moe_combine_ln(xf, y, route, slots, ln_ffn_g[i], ln_ffn_b[i], alpha, tm=comb_rows)
        xf, xb = _ple(xb, xf, p3, w_ple_gate, w_ple, i)
    return xf.reshape(batch, seq, d_model)
```

```python
import functools

import jax
import jax.numpy as jnp
from jax import lax
from jax.experimental import pallas as pl
from jax.experimental.pallas import tpu as pltpu

F32 = jnp.float32
BF16 = jnp.bfloat16
FP8 = jnp.float8_e4m3fn
FP8_PEAK = 256.0

CHUNK = 64
N_GROUPS = 4
RET_HEADS = 8
SB_HEADS = 16
ROPE_BASE = 10000.0
LN_EPS = 1e-5
NEG = -1e30

LANE = 128
SUBLANE = 8
VMEM_LIMIT_BYTES = 56 * 1024 * 1024


def _cparams(n_axes):
    return pltpu.CompilerParams(dimension_semantics=("arbitrary",) * n_axes,
                                vmem_limit_bytes=VMEM_LIMIT_BYTES)


def _sigmoid(x):
    return 1.0 / (1.0 + jnp.exp(-x))


def _dot(a, b):
    return jnp.dot(a, b, preferred_element_type=F32)


def _dot_nt(a, b):
    return lax.dot_general(a, b, (((1,), (1,)), ((), ())), preferred_element_type=F32)


def _mm_kernel(a_ref, b_ref, o_ref):
    o_ref[...] = _dot(a_ref[...], b_ref[...]).astype(o_ref.dtype)


def _tile(n, pref):
    t = min(pref, n)
    while n % t or (t % LANE and t != n):
        t -= LANE if t % LANE == 0 else t % LANE
    return t


def _matmul(a, b, out_dtype, tm=1024, tn=512):
    m, kd = a.shape
    n = b.shape[1]
    tm, tn = _tile(m, tm), _tile(n, tn)
    return pl.pallas_call(
        _mm_kernel,
        grid=(m // tm, n // tn),
        in_specs=[pl.BlockSpec((tm, kd), lambda i, j: (i, 0)),
                  pl.BlockSpec((kd, tn), lambda i, j: (0, j))],
        out_specs=pl.BlockSpec((tm, tn), lambda i, j: (i, j)),
        out_shape=jax.ShapeDtypeStruct((m, n), out_dtype),
        compiler_params=_cparams(2),
        name="matmul",
    )(a, b)


def _quant_rows_store(x, q_ref, s_ref):
    amax = jnp.max(jnp.abs(x), axis=-1, keepdims=True)
    scale = jnp.where(amax > 0.0, amax * (1.0 / FP8_PEAK), 1.0)
    q_ref[...] = (x / scale).astype(FP8)
    s_ref[...] = jnp.broadcast_to(scale, s_ref.shape)


def _quant_rows_kernel(x_ref, q_ref, s_ref):
    _quant_rows_store(x_ref[...].astype(F32), q_ref, s_ref)


def _quant_rows(x, tm=512):
    m, kd = x.shape
    tm = _tile(m, tm)
    return pl.pallas_call(
        _quant_rows_kernel,
        grid=(m // tm,),
        in_specs=[pl.BlockSpec((tm, kd), lambda i: (i, 0))],
        out_specs=[pl.BlockSpec((tm, kd), lambda i: (i, 0)), pl.BlockSpec((tm, LANE), lambda i: (i, 0))],
        out_shape=[jax.ShapeDtypeStruct((m, kd), FP8), jax.ShapeDtypeStruct((m, LANE), F32)],
        compiler_params=_cparams(1),
        name="quant_rows",
    )(x)


def _quant_cols(w_ref, col_mult, wq_ref, ws_ref):
    kd, tn = wq_ref.shape
    chunk = _tile(kd, 512)

    def amax_body(c, acc):
        w = w_ref[0, pl.ds(pl.multiple_of(c * chunk, chunk), chunk), :]
        return jnp.maximum(acc, jnp.max(jnp.abs(w), axis=0, keepdims=True))

    amax = lax.fori_loop(0, kd // chunk, amax_body, jnp.zeros((1, tn), F32)) * jnp.abs(col_mult)
    scale = jnp.where(amax > 0.0, amax * (1.0 / FP8_PEAK), 1.0)
    mult = col_mult / scale

    def quant_body(c, carry):
        rows = pl.ds(pl.multiple_of(c * chunk, chunk), chunk)
        wq_ref[rows, :] = (w_ref[0, rows, :] * mult).astype(FP8)
        return carry

    lax.fori_loop(0, kd // chunk, quant_body, 0)
    ws_ref[...] = jnp.broadcast_to(scale, ws_ref.shape)


def _proj_in_kernel(a_ref, r_ref, w_ref, s_ref, o_ref, wq_ref, ws_ref):
    @pl.when(pl.program_id(1) == 0)
    def _():
        _quant_cols(w_ref, s_ref[...], wq_ref, ws_ref)

    o_ref[...] = (_dot(a_ref[...], wq_ref[...]) * r_ref[:, 0:1] * ws_ref[0:1, :]).astype(o_ref.dtype)


def _proj_in(a8, a_scale, w_all, layer, n_cols, col_scale, tm=1024, tn=1024):
    m, kd = a8.shape
    tm, tn = _tile(m, tm), _tile(n_cols, tn)
    return pl.pallas_call(
        _proj_in_kernel,
        grid=(n_cols // tn, m // tm),
        in_specs=[pl.BlockSpec((tm, kd), lambda n, i: (i, 0)),
                  pl.BlockSpec((tm, LANE), lambda n, i: (i, 0)),
                  pl.BlockSpec((1, kd, tn), lambda n, i: (layer, 0, n)),
                  pl.BlockSpec((1, tn), lambda n, i: (0, n))],
        out_specs=pl.BlockSpec((tm, tn), lambda n, i: (i, n)),
        out_shape=jax.ShapeDtypeStruct((m, n_cols), BF16),
        scratch_shapes=[pltpu.VMEM((kd, tn), FP8), pltpu.VMEM((SUBLANE, tn), F32)],
        compiler_params=_cparams(2),
        name="proj_in",
    )(a8, a_scale, w_all, col_scale.reshape(1, n_cols))


def _proj_out_kernel(a1_ref, r1_ref, a2_ref, r2_ref, w1_ref, w2_ref, o_ref, wq1_ref, ws1_ref, wq2_ref, ws2_ref):
    @pl.when(pl.program_id(1) == 0)
    def _():
        _quant_cols(w1_ref, jnp.ones((1, wq1_ref.shape[1]), F32), wq1_ref, ws1_ref)
        _quant_cols(w2_ref, jnp.ones((1, wq2_ref.shape[1]), F32), wq2_ref, ws2_ref)

    o_ref[...] = (_dot(a1_ref[...], wq1_ref[...]) * r1_ref[:, 0:1] * ws1_ref[0:1, :]
                  + _dot(a2_ref[...], wq2_ref[...]) * r2_ref[:, 0:1] * ws2_ref[0:1, :])


def _proj_out(a1, r1, a2, r2, w_all, layer, tm=1024, tn=512):
    m, k1 = a1.shape
    n = w_all.shape[2]
    assert a2.shape[1] == k1 and w_all.shape[1] == 2 * k1
    tm, tn = _tile(m, tm), _tile(n, tn)
    act = pl.BlockSpec((tm, k1), lambda c, i: (i, 0))
    rsc = pl.BlockSpec((tm, LANE), lambda c, i: (i, 0))
    return pl.pallas_call(
        _proj_out_kernel,
        grid=(n // tn, m // tm),
        in_specs=[act, rsc, act, rsc,
                  pl.BlockSpec((1, k1, tn), lambda c, i: (layer, 0, c)),
                  pl.BlockSpec((1, k1, tn), lambda c, i: (layer, 1, c))],
        out_specs=pl.BlockSpec((tm, tn), lambda c, i: (i, c)),
        out_shape=jax.ShapeDtypeStruct((m, n), F32),
        scratch_shapes=[pltpu.VMEM((k1, tn), FP8), pltpu.VMEM((SUBLANE, tn), F32),
                        pltpu.VMEM((k1, tn), FP8), pltpu.VMEM((SUBLANE, tn), F32)],
        compiler_params=_cparams(2),
        name="proj_out",
    )(a1, r1, a2, r2, w_all, w_all)


def _layernorm_rows(t, g, b):
    mu = jnp.mean(t, axis=-1, keepdims=True)
    d = t - mu
    var = jnp.mean(d * d, axis=-1, keepdims=True)
    return d * lax.rsqrt(var + LN_EPS) * g + b


def _res_ln_kernel(x_ref, y_ref, g_ref, b_ref, of_ref, ob_ref, *, alpha):
    o = _layernorm_rows(alpha * x_ref[...] + y_ref[...], g_ref[...], b_ref[...])
    of_ref[...] = o
    ob_ref[...] = o.astype(BF16)


def _res_ln(x, y, g, b, alpha, tm=256):
    m, d = x.shape
    tm = min(tm, m)
    row = pl.BlockSpec((tm, d), lambda i: (i, 0))
    vec = pl.BlockSpec((1, d), lambda i: (0, 0))
    return pl.pallas_call(
        functools.partial(_res_ln_kernel, alpha=alpha),
        grid=(m // tm,),
        in_specs=[row, row, vec, vec],
        out_specs=[row, row],
        out_shape=[jax.ShapeDtypeStruct((m, d), F32), jax.ShapeDtypeStruct((m, d), BF16)],
        compiler_params=_cparams(1),
        name="res_ln",
    )(x, y, g.reshape(1, d), b.reshape(1, d))


def _conv_kernel(a_ref, g_ref, ap_ref, gp_ref, cw_ref, cb_ref, lg_ref, lb_ref, o_ref, os_ref, buf_ref, y_ref,
                 *, t_rows, halo, width):
    i = pl.program_id(1)
    ch = a_ref.shape[1]
    buf_ref[pl.ds(halo, t_rows), :] = a_ref[...].astype(F32) * _sigmoid(g_ref[...].astype(F32))
    ap = ap_ref[pl.ds(t_rows - halo, halo), :].astype(F32)
    gp = gp_ref[pl.ds(t_rows - halo, halo), :].astype(F32)
    tail = ap * _sigmoid(gp)
    buf_ref[pl.ds(0, halo), :] = jnp.where(i > 0, tail, 0.0)
    base = halo - (width - 1)
    for c0 in range(0, ch, LANE):
        acc = jnp.zeros((t_rows, LANE), F32) + cb_ref[:, c0:c0 + LANE]
        for r in range(SUBLANE):
            taps = [w for w in range(width) if (base + w) % SUBLANE == r]
            if not taps:
                continue
            rows = t_rows + (SUBLANE if r else 0)
            z = None
            for w in taps:
                term = buf_ref[pl.ds(base + w - r, rows), c0:c0 + LANE] * cw_ref[w:w + 1, c0:c0 + LANE]
                z = term if z is None else z + term
            acc = acc + z[r:r + t_rows, :]
        y_ref[:, c0:c0 + LANE] = acc
    y = _layernorm_rows(y_ref[...], lg_ref[...], lb_ref[...])
    _quant_rows_store(y * _sigmoid(y), o_ref, os_ref)


def _conv_module(h, batch, seq, ch, conv_w, conv_b, ln_g, ln_b, t_rows=256, halo=32):
    width = conv_w.shape[0]
    t_rows = min(t_rows, seq)
    assert width - 1 <= halo <= t_rows and seq % t_rows == 0 and halo % SUBLANE == 0
    nt = seq // t_rows
    cur = lambda col: pl.BlockSpec((t_rows, ch), lambda b, i: (b * nt + i, col))
    prev = lambda col: pl.BlockSpec((t_rows, ch), lambda b, i: (b * nt + jnp.maximum(i - 1, 0), col))
    vec = pl.BlockSpec((1, ch), lambda b, i: (0, 0))
    return pl.pallas_call(
        functools.partial(_conv_kernel, t_rows=t_rows, halo=halo, width=width),
        grid=(batch, nt),
        in_specs=[cur(0), cur(1), prev(0), prev(1),
                  pl.BlockSpec((width, ch), lambda b, i: (0, 0)), vec, vec, vec],
        out_specs=[pl.BlockSpec((t_rows, ch), lambda b, i: (b * nt + i, 0)),
                   pl.BlockSpec((t_rows, LANE), lambda b, i: (b * nt + i, 0))],
        out_shape=[jax.ShapeDtypeStruct((batch * seq, ch), FP8), jax.ShapeDtypeStruct((batch * seq, LANE), F32)],
        scratch_shapes=[pltpu.VMEM((t_rows + halo, ch), F32), pltpu.VMEM((t_rows, ch), F32)],
        compiler_params=_cparams(2),
        name="conv_module",
    )(h, h, h, h, conv_w, conv_b.reshape(1, ch), ln_g.reshape(1, ch), ln_b.reshape(1, ch))


def _cum_kernel(f_ref, b_ref, o_ref):
    x = f_ref[...] + b_ref[...]
    ls = jnp.minimum(x, 0.0) - jnp.log(1.0 + jnp.exp(-jnp.abs(x)))
    lane = lax.broadcasted_iota(jnp.int32, ls.shape, 1)
    sh = 1
    while sh < ls.shape[1]:
        ls = ls + jnp.where(lane >= sh, pltpu.roll(ls, sh, 1), 0.0)
        sh *= 2
    o_ref[...] = ls


def _fox_cum(f_rows, bias_rows):
    r, s = f_rows.shape
    return pl.pallas_call(
        _cum_kernel,
        grid=(1,),
        in_specs=[pl.BlockSpec((r, s), lambda i: (0, 0)), pl.BlockSpec((r, 1), lambda i: (0, 0))],
        out_specs=pl.BlockSpec((r, s), lambda i: (0, 0)),
        out_shape=jax.ShapeDtypeStruct((r, s), F32),
        compiler_params=_cparams(1),
        name="fox_cum",
    )(f_rows, bias_rows)


def _fox_kernel(q_ref, k_ref, v_ref, c_ref, o_ref, *, blk):
    i = pl.program_id(2)
    hd = q_ref.shape[1]
    q = q_ref[...]
    row = lax.broadcasted_iota(jnp.int32, (blk, blk), 0)
    col = lax.broadcasted_iota(jnp.int32, (blk, blk), 1)

    def step(kj, carry, masked):
        m, l, acc = carry
        start = pl.multiple_of(kj * blk, blk)
        k = k_ref[pl.ds(start, blk), :]
        v = v_ref[pl.ds(start, blk), :]
        s = _dot_nt(q, k) - c_ref[0, kj]
        if masked:
            s = jnp.where(col <= row, s, NEG)
        m_new = jnp.maximum(m, jnp.max(s, axis=-1, keepdims=True))
        p = jnp.exp(s - m_new)
        alpha = jnp.exp(m - m_new)
        l = alpha * l + jnp.sum(p, axis=-1, keepdims=True)
        acc = alpha * acc + _dot(p.astype(BF16), v)
        return m_new, l, acc

    init = (jnp.full((blk, 1), NEG, F32), jnp.zeros((blk, 1), F32), jnp.zeros((blk, hd), F32))
    carry = lax.fori_loop(0, i, lambda kj, c: step(kj, c, False), init)
    _, l, acc = step(i, carry, True)
    o_ref[...] = (acc / l).astype(o_ref.dtype)


def _fox_attention(h, cum, batch, seq, n_heads, hd, q_col, k_col, v_col, blk=512):
    blk = min(blk, seq)
    nq = seq // blk
    return pl.pallas_call(
        functools.partial(_fox_kernel, blk=blk),
        grid=(batch, n_heads, nq),
        in_specs=[pl.BlockSpec((blk, hd), lambda b, hh, i: (b * nq + i, q_col + hh)),
                  pl.BlockSpec((seq, hd), lambda b, hh, i: (b, k_col + hh)),
                  pl.BlockSpec((seq, hd), lambda b, hh, i: (b, v_col + hh)),
                  pl.BlockSpec((1, nq, 1, blk), lambda b, hh, i: (b * n_heads + hh, 0, 0, 0))],
        out_specs=pl.BlockSpec((blk, hd), lambda b, hh, i: (b * nq + i, hh)),
        out_shape=jax.ShapeDtypeStruct((batch * seq, n_heads * hd), BF16),
        compiler_params=_cparams(3),
        name="fox_attention",
    )(h, h, h, cum)


SB_DEAD = 105.0


def _sb_kernel(q_ref, k_ref, v_ref, o_ref, *, blk, heads, hd):
    i = pl.program_id(2)
    jr = lax.broadcasted_iota(jnp.int32, (blk, blk), 0)
    jc = lax.broadcasted_iota(jnp.int32, (blk, blk), 1)
    upper = jnp.where(jr > jc, 1.0, 0.0).astype(BF16)
    strict = jc < jr

    def block(hh, start, run, acc, masked):
        cols = slice(hh * hd, (hh + 1) * hd)
        z = _dot_nt(q_ref[:, cols], k_ref[pl.ds(start, blk), cols])
        sp = jnp.maximum(z, 0.0) + jnp.log(1.0 + jnp.exp(-jnp.abs(z)))
        log_1m = -sp
        log_b = z - sp
        if masked:
            log_1m = jnp.where(strict, log_1m, 0.0)
        hi = log_1m.astype(BF16)
        lo = (log_1m - hi.astype(F32)).astype(BF16)
        later = _dot(hi, upper) + _dot(lo, upper)
        w = jnp.exp(log_b + later + run)
        if masked:
            w = jnp.where(strict, w, 0.0)
        acc = acc + _dot(w.astype(BF16), v_ref[pl.ds(start, blk), cols])
        return run + later[:, 0:1] + log_1m[:, 0:1], acc

    def step(kj, carry, masked):
        start = pl.multiple_of(kj * blk, blk)
        return tuple(block(hh, start, *carry[hh], masked) for hh in range(heads))

    def live(carry):
        return functools.reduce(jnp.maximum, [jnp.max(run) for run, _ in carry])

    carry = tuple((jnp.zeros((blk, 1), F32), jnp.zeros((blk, hd), F32)) for _ in range(heads))
    carry = step(i, carry, True)

    def cond(state):
        t, top, _ = state
        return jnp.logical_and(t < i, top > -SB_DEAD)

    def body(state):
        t, _, c = state
        c = step(i - 1 - t, c, False)
        return t + 1, live(c), c

    _, _, carry = lax.while_loop(cond, body, (jnp.int32(0), live(carry), carry))
    for hh in range(heads):
        o_ref[:, hh * hd:(hh + 1) * hd] = carry[hh][1].astype(o_ref.dtype)


def _sb_attention(h, batch, seq, n_heads, hd, q_col, k_col, v_col, blk=256, heads=2):
    blk = min(blk, seq)
    nq = seq // blk
    assert n_heads % heads == 0 and q_col % heads == 0 and k_col % heads == 0 and v_col % heads == 0
    wide = heads * hd
    return pl.pallas_call(
        functools.partial(_sb_kernel, blk=blk, heads=heads, hd=hd),
        grid=(batch, n_heads // heads, nq),
        in_specs=[pl.BlockSpec((blk, wide), lambda b, hh, i: (b * nq + i, q_col // heads + hh)),
                  pl.BlockSpec((seq, wide), lambda b, hh, i: (b, k_col // heads + hh)),
                  pl.BlockSpec((seq, wide), lambda b, hh, i: (b, v_col // heads + hh))],
        out_specs=pl.BlockSpec((blk, wide), lambda b, hh, i: (b * nq + i, hh)),
        out_shape=jax.ShapeDtypeStruct((batch * seq, n_heads * hd), BF16),
        compiler_params=_cparams(3),
        name="sb_attention",
    )(h, h, h)


def _rope_kernel(pos_ref, inv_ref, cos_ref, sin_ref):
    ang = pos_ref[...] * inv_ref[...]
    lane = lax.broadcasted_iota(jnp.int32, ang.shape, 1)
    cos_ref[...] = jnp.cos(ang)
    sin_ref[...] = jnp.where((lane & 1) == 0, -jnp.sin(ang), jnp.sin(ang))


def _rope_tables(pos_col, inv_row, tm=512):
    m = pos_col.shape[0]
    half = inv_row.shape[1]
    tm = min(tm, m)
    out = pl.BlockSpec((tm, half), lambda i: (i, 0))
    return pl.pallas_call(
        _rope_kernel,
        grid=(m // tm,),
        in_specs=[pl.BlockSpec((tm, 1), lambda i: (i, 0)), pl.BlockSpec((1, half), lambda i: (0, 0))],
        out_specs=[out, out],
        out_shape=[jax.ShapeDtypeStruct((m, half), F32)] * 2,
        compiler_params=_cparams(1),
        name="rope_tables",
    )(pos_col, inv_row)


def _ret_kernel(lg_ref, q_ref, k_ref, v_ref, gate_ref, cos_ref, sin_ref, gn_ref, o_ref, state_ref, *, blk):
    hh = pl.program_id(1)
    i = pl.program_id(2)
    d = q_ref.shape[1]
    lg = lg_ref[hh]

    @pl.when(i == 0)
    def _():
        state_ref[...] = jnp.zeros_like(state_ref)

    cos = cos_ref[...]
    sin = sin_ref[...]
    even = (lax.broadcasted_iota(jnp.int32, (blk, d), 1) & 1) == 0

    def rot(t_ref):
        t = t_ref[...].astype(F32)
        partner = jnp.where(even, pltpu.roll(t, d - 1, 1), pltpu.roll(t, 1, 1))
        return t * cos + partner * sin

    qr = rot(q_ref)
    kr = rot(k_ref)
    v = v_ref[...]
    n = lax.broadcasted_iota(jnp.int32, (blk, 1), 0).astype(F32)
    q_decay = jnp.exp((n + 1.0) * lg)
    k_decay = jnp.exp((blk - 1.0 - n) * lg)
    row = lax.broadcasted_iota(jnp.int32, (blk, blk), 0)
    col = lax.broadcasted_iota(jnp.int32, (blk, blk), 1)
    dist = jnp.abs(row - col).astype(F32)
    shift = CHUNK.bit_length() - 1
    seen = (col >> shift) <= (row >> shift)
    decay = jnp.where(seen, jnp.exp(dist * lg), 0.0)
    scores = _dot_nt(qr.astype(BF16), kr.astype(BF16)) * decay
    out = _dot(scores.astype(BF16), v) + _dot((qr * q_decay).astype(BF16), state_ref[...].astype(BF16))
    kv = lax.dot_general((kr * k_decay).astype(BF16), v, (((0,), (0,)), ((), ())), preferred_element_type=F32)
    state_ref[...] = jnp.exp(jnp.full((1, d), blk * lg, F32)) * state_ref[...] + kv
    mu = jnp.mean(out, axis=-1, keepdims=True)
    dv = out - mu
    var = jnp.mean(dv * dv, axis=-1, keepdims=True)
    y = dv * lax.rsqrt(var + LN_EPS) * gn_ref[...]
    g = gate_ref[...].astype(F32)
    o_ref[...] = (y * (g * _sigmoid(g))).astype(o_ref.dtype)


def _retention(h, cos, sin, log_g, gn, batch, seq, n_heads, d, blk=256):
    blk = min(blk, seq)
    assert blk % CHUNK == 0 and CHUNK & (CHUNK - 1) == 0
    nb = seq // blk
    hcol = lambda off: pl.BlockSpec((blk, d), lambda b, hh, i: (b * nb + i, off + hh))
    tab = pl.BlockSpec((blk, d), lambda b, hh, i: (b * nb + i, 0))
    return pl.pallas_call(
        functools.partial(_ret_kernel, blk=blk),
        grid=(batch, n_heads, nb),
        in_specs=[pl.BlockSpec(memory_space=pltpu.SMEM),
                  hcol(0), hcol(n_heads), hcol(2 * n_heads), hcol(3 * n_heads), tab, tab,
                  pl.BlockSpec((1, d), lambda b, hh, i: (0, hh))],
        out_specs=pl.BlockSpec((blk, d), lambda b, hh, i: (b * nb + i, hh)),
        out_shape=jax.ShapeDtypeStruct((batch * seq, n_heads * d), BF16),
        scratch_shapes=[pltpu.VMEM((d, d), F32)],
        compiler_params=_cparams(3),
        name="retention",
    )(log_g, h, h, h, h, cos, sin, gn.reshape(1, n_heads * d))


def _router_kernel(x_ref, w_ref, b_ref, o_ref, *, n_exp, epg):
    logits = _dot(x_ref[...], w_ref[...]) + b_ref[...]
    lane = lax.broadcasted_iota(jnp.int32, logits.shape, 1)
    lmax = jnp.max(logits, axis=-1, keepdims=True)
    p = jnp.where(lane < n_exp, jnp.exp(logits - lmax), -1.0)
    p1 = jnp.max(p, axis=-1, keepdims=True)
    e1 = jnp.min(jnp.where(p == p1, lane, LANE), axis=-1, keepdims=True)
    shift = epg.bit_length() - 1
    in_group = (lane >> shift) == (e1 >> shift)
    cand = jnp.where(in_group, jnp.where(lane == e1, -1.0, p), -1.0)
    p2 = jnp.max(cand, axis=-1, keepdims=True)
    e2 = jnp.min(jnp.where(cand == p2, lane, LANE), axis=-1, keepdims=True)
    tot = p1 + p2
    o_ref[...] = jnp.where(lane == 0, e1.astype(F32),
                           jnp.where(lane == 1, e2.astype(F32),
                                     jnp.where(lane == 2, p1 / tot, jnp.where(lane == 3, p2 / tot, 0.0))))


def _router(xb, w_pad, b_pad, n_exp, tm=512):
    m, d = xb.shape
    tm = min(tm, m)
    epg = n_exp // N_GROUPS
    assert epg & (epg - 1) == 0 and n_exp <= LANE
    return pl.pallas_call(
        functools.partial(_router_kernel, n_exp=n_exp, epg=epg),
        grid=(m // tm,),
        in_specs=[pl.BlockSpec((tm, d), lambda i: (i, 0)),
                  pl.BlockSpec((d, LANE), lambda i: (0, 0)),
                  pl.BlockSpec((1, LANE), lambda i: (0, 0))],
        out_specs=pl.BlockSpec((tm, LANE), lambda i: (i, 0)),
        out_shape=jax.ShapeDtypeStruct((m, LANE), F32),
        compiler_params=_cparams(1),
        name="router",
    )(xb, w_pad, b_pad)


GATHER_UNROLL = 8


def _row_copy(src_hbm, row, dst_ref, r, sem):
    return pltpu.make_async_copy(src_hbm.at[pl.ds(row, 1), :], dst_ref.at[pl.ds(r, 1), :], sem)


def _gather_start(idx_ref, n_rows, src_hbm, dst_ref, sem):
    def issue(r, carry):
        _row_copy(src_hbm, idx_ref[0, 0, r], dst_ref, r, sem).start()
        return carry
    lax.fori_loop(0, n_rows, issue, 0, unroll=GATHER_UNROLL)


def _gather_wait(n_rows, src_hbm, dst_ref, sem):
    def drain(r, carry):
        _row_copy(src_hbm, 0, dst_ref, r, sem).wait()
        return carry
    lax.fori_loop(0, n_rows, drain, 0, unroll=GATHER_UNROLL)


def _moe_kernel(te_ref, tv_ref, cur_ref, nxt_ref, x_hbm, wg_ref, wu_ref, wd_ref, y_ref, xbuf, sem, *, t_rows):
    t = pl.program_id(0)
    slot = t % 2

    @pl.when(t == 0)
    def _():
        _gather_start(cur_ref, t_rows, x_hbm, xbuf.at[0], sem.at[0])

    @pl.when(jnp.logical_and(t + 1 < pl.num_programs(0), tv_ref[jnp.minimum(t + 1, pl.num_programs(0) - 1)] == 1))
    def _():
        _gather_start(nxt_ref, t_rows, x_hbm, xbuf.at[1 - slot], sem.at[1 - slot])

    @pl.when(tv_ref[t] == 1)
    def _():
        _gather_wait(t_rows, x_hbm, xbuf.at[slot], sem.at[slot])
        xb = xbuf[slot].astype(BF16)
        hg = _dot(xb, wg_ref[0, 0])
        hu = _dot(xb, wu_ref[0, 0])
        hidden = (hg * _sigmoid(hg) * hu).astype(BF16)
        y_ref[...] = _dot(hidden, wd_ref[0, 0])

    @pl.when(tv_ref[t] == 0)
    def _():
        y_ref[...] = jnp.zeros_like(y_ref)


def _moe_ffn(x, src_rows, tile_expert, tile_valid, w_gate, w_up, w_down, layer, t_rows):
    n_tiles = tile_expert.shape[0]
    d = x.shape[1]
    f = w_gate.shape[3]
    wspec = lambda a, b: pl.BlockSpec((1, 1, a, b), lambda t, te, tv: (layer, te[t], 0, 0))
    grid_spec = pltpu.PrefetchScalarGridSpec(
        num_scalar_prefetch=2,
        grid=(n_tiles,),
        in_specs=[pl.BlockSpec((1, 1, t_rows), lambda t, te, tv: (t, 0, 0), memory_space=pltpu.SMEM),
                  pl.BlockSpec((1, 1, t_rows), lambda t, te, tv: (jnp.minimum(t + 1, n_tiles - 1), 0, 0),
                               memory_space=pltpu.SMEM),
                  pl.BlockSpec(memory_space=pl.ANY),
                  wspec(d, f), wspec(d, f), wspec(f, d)],
        out_specs=pl.BlockSpec((t_rows, d), lambda t, te, tv: (t, 0)),
        scratch_shapes=[pltpu.VMEM((2, t_rows, d), F32), pltpu.SemaphoreType.DMA((2,))],
    )
    src3 = src_rows.reshape(n_tiles, 1, t_rows)
    return pl.pallas_call(
        functools.partial(_moe_kernel, t_rows=t_rows),
        grid_spec=grid_spec,
        out_shape=jax.ShapeDtypeStruct((n_tiles * t_rows, d), F32),
        compiler_params=_cparams(1),
        name="moe_ffn",
    )(tile_expert, tile_valid, src3, src3, x, w_gate, w_up, w_down)


def _combine_kernel(cur_ref, nxt_ref, route_ref, x_ref, y_hbm, g_ref, b_ref, of_ref, oq_ref, os_ref, buf, sem,
                    *, tm, alpha):
    i = pl.program_id(0)
    slot = i % 2

    @pl.when(i == 0)
    def _():
        _gather_start(cur_ref, 2 * tm, y_hbm, buf.at[0], sem.at[0])

    @pl.when(i + 1 < pl.num_programs(0))
    def _():
        _gather_start(nxt_ref, 2 * tm, y_hbm, buf.at[1 - slot], sem.at[1 - slot])

    _gather_wait(2 * tm, y_hbm, buf.at[slot], sem.at[slot])
    w1 = route_ref[:, 2:3]
    w2 = route_ref[:, 3:4]
    ffn = w1 * buf[slot, pl.ds(0, tm), :] + w2 * buf[slot, pl.ds(tm, tm), :]
    o = _layernorm_rows(alpha * x_ref[...] + ffn, g_ref[...], b_ref[...])
    of_ref[...] = o
    _quant_rows_store(o, oq_ref, os_ref)


def _moe_combine_ln(x, y, route, slots, g, b, alpha, tm=128):
    m, d = x.shape
    row = pl.BlockSpec((tm, d), lambda i: (i, 0))
    vec = pl.BlockSpec((1, d), lambda i: (0, 0))
    return pl.pallas_call(
        functools.partial(_combine_kernel, tm=tm, alpha=alpha),
        grid=(m // tm,),
        in_specs=[pl.BlockSpec((1, 1, 2 * tm), lambda i: (i, 0, 0), memory_space=pltpu.SMEM),
                  pl.BlockSpec((1, 1, 2 * tm), lambda i: (jnp.minimum(i + 1, m // tm - 1), 0, 0),
                               memory_space=pltpu.SMEM),
                  pl.BlockSpec((tm, LANE), lambda i: (i, 0)),
                  row,
                  pl.BlockSpec(memory_space=pl.ANY),
                  vec, vec],
        out_specs=[row, row, pl.BlockSpec((tm, LANE), lambda i: (i, 0))],
        out_shape=[jax.ShapeDtypeStruct((m, d), F32), jax.ShapeDtypeStruct((m, d), FP8),
                   jax.ShapeDtypeStruct((m, LANE), F32)],
        scratch_shapes=[pltpu.VMEM((2, 2 * tm, d), F32), pltpu.SemaphoreType.DMA((2,))],
        compiler_params=_cparams(1),
        name="moe_combine_ln",
    )(slots, slots, route, x, y, g.reshape(1, d), b.reshape(1, d))


def _slot_kernel(route_ref, slot_ref, ends_ref, cnt_ref, run_ref, start_ref, *, tb, t_rows):
    ph = pl.program_id(0)
    i = pl.program_id(1)
    r = route_ref[...]
    lane = lax.broadcasted_iota(jnp.int32, (tb, LANE), 1)
    oh1 = lane == r[:, 0:1].astype(jnp.int32)
    oh2 = lane == r[:, 1:2].astype(jnp.int32)
    both = jnp.where(oh1, 1.0, 0.0) + jnp.where(oh2, 1.0, 0.0)
    col_sum = jnp.sum(both, axis=0, keepdims=True)

    @pl.when(jnp.logical_and(ph == 0, i == 0))
    def _():
        cnt_ref[...] = jnp.zeros_like(cnt_ref)

    @pl.when(ph == 0)
    def _():
        cnt_ref[...] = cnt_ref[...] + col_sum

    @pl.when(jnp.logical_and(ph == 1, i == 0))
    def _():
        shift = t_rows.bit_length() - 1
        padded = ((cnt_ref[...].astype(jnp.int32) + (t_rows - 1)) >> shift) << shift
        lane8 = lax.broadcasted_iota(jnp.int32, padded.shape, 1)
        ends = padded
        sh = 1
        while sh < LANE:
            ends = ends + jnp.where(lane8 >= sh, pltpu.roll(ends, sh, 1), 0)
            sh *= 2
        start_ref[...] = (ends - padded).astype(F32)
        ends_ref[...] = ends.astype(F32)
        run_ref[...] = jnp.zeros_like(run_ref)

    @pl.when(ph == 1)
    def _():
        tr = lax.broadcasted_iota(jnp.int32, (tb, tb), 0)
        tc = lax.broadcasted_iota(jnp.int32, (tb, tb), 1)
        earlier = jnp.where(tr > tc, 1.0, 0.0).astype(BF16)
        before = _dot(earlier, both.astype(BF16)) + run_ref[0:1, :] + start_ref[0:1, :]
        s1 = jnp.sum(jnp.where(oh1, before, 0.0), axis=-1, keepdims=True)
        s2 = jnp.sum(jnp.where(oh2, before, 0.0), axis=-1, keepdims=True)
        slot_ref[...] = jnp.where(lane == 0, s1, jnp.where(lane == 1, s2, 0.0))
        run_ref[...] = run_ref[...] + col_sum


def _moe_slots(route, t_rows, tb=512):
    m = route.shape[0]
    tb = min(tb, m)
    assert t_rows & (t_rows - 1) == 0
    acc = pltpu.VMEM((8, LANE), F32)
    return pl.pallas_call(
        functools.partial(_slot_kernel, tb=tb, t_rows=t_rows),
        grid=(2, m // tb),
        in_specs=[pl.BlockSpec((tb, LANE), lambda ph, i: (i, 0))],
        out_specs=[pl.BlockSpec((tb, LANE), lambda ph, i: (i * ph, 0)),
                   pl.BlockSpec((8, LANE), lambda ph, i: (0, 0))],
        out_shape=[jax.ShapeDtypeStruct((m, LANE), F32), jax.ShapeDtypeStruct((8, LANE), F32)],
        scratch_shapes=[acc, acc, acc],
        compiler_params=_cparams(2),
        name="moe_slots",
    )(route)


def _routing_tables(route, n_exp, t_rows, tm):
    m = route.shape[0]
    slots_f, ends_f = _moe_slots(route, t_rows)
    slot2 = slots_f[:, 0:2].astype(jnp.int32)
    ends = ends_f[0, :n_exp].astype(jnp.int32)
    n_slots = 2 * m + n_exp * t_rows
    n_tiles = n_slots // t_rows
    src_rows = jnp.zeros((n_slots,), jnp.int32).at[slot2.reshape(-1)].set(jnp.arange(2 * m, dtype=jnp.int32) // 2)
    tile_start = jnp.arange(n_tiles, dtype=jnp.int32) * t_rows
    tile_valid = (tile_start < ends[-1]).astype(jnp.int32)
    last_valid = jnp.maximum(ends[-1] - 1, 0)
    probe = jnp.minimum(tile_start, last_valid)
    tile_expert = jnp.sum((ends[None, :] <= probe[:, None]).astype(jnp.int32), axis=1)
    tile_expert = jnp.minimum(tile_expert, n_exp - 1)
    slots = jnp.concatenate([slot2[:, 0].reshape(m // tm, 1, tm), slot2[:, 1].reshape(m // tm, 1, tm)], axis=2)
    return src_rows, tile_expert, tile_valid, slots


def _ple_kernel(a_ref, r_ref, w_ref, x_ref, p_ref, wp_ref, of_ref, ob_ref, wq_ref, ws_ref):
    @pl.when(pl.program_id(1) == 0)
    def _():
        _quant_cols(w_ref, jnp.ones((1, wq_ref.shape[1]), F32), wq_ref, ws_ref)

    gate = _sigmoid(_dot(a_ref[...], wq_ref[...]) * r_ref[:, 0:1] * ws_ref[0:1, :])
    o = x_ref[...] + gate * _dot(p_ref[0].astype(BF16), wp_ref[0].astype(BF16))
    of_ref[...] = o
    ob_ref[...] = o.astype(BF16)


def _ple(a8, a_scale, xf, p_all, w_gate_all, w_ple_all, layer, tm=1024, tn=512):
    m, d = a8.shape
    pd = p_all.shape[2]
    tm, tn = _tile(m, tm), _tile(d, tn)
    out = pl.BlockSpec((tm, tn), lambda c, i: (i, c))
    return pl.pallas_call(
        _ple_kernel,
        grid=(d // tn, m // tm),
        in_specs=[pl.BlockSpec((tm, d), lambda c, i: (i, 0)),
                  pl.BlockSpec((tm, LANE), lambda c, i: (i, 0)),
                  pl.BlockSpec((1, d, tn), lambda c, i: (layer, 0, c)),
                  out,
                  pl.BlockSpec((1, tm, pd), lambda c, i: (layer, i, 0)),
                  pl.BlockSpec((1, pd, tn), lambda c, i: (layer, 0, c))],
        out_specs=[out, out],
        out_shape=[jax.ShapeDtypeStruct((m, d), F32), jax.ShapeDtypeStruct((m, d), BF16)],
        scratch_shapes=[pltpu.VMEM((d, tn), FP8), pltpu.VMEM((SUBLANE, tn), F32)],
        compiler_params=_cparams(2),
        name="ple",
    )(a8, a_scale, w_gate_all, xf, p_all, w_ple_all)


def kernel(x, p, positions, w_in_even, conv_w, conv_b, conv_ln_g, conv_ln_b, fox_f_bias, w_out_even, w_in_odd, ret_norm_g, w_out_odd, ln_mix_g, ln_mix_b, ln_ffn_g, ln_ffn_b, w_router, b_router, w_gate, w_up, w_down, w_ple, w_ple_gate):
    batch, seq, d_model = x.shape
    depth = ln_mix_g.shape[0]
    m = batch * seq
    gw = d_model // 2
    ch = conv_w.shape[2]
    fox_heads = fox_f_bias.shape[1]
    fox_hd = gw // fox_heads
    ret_hd = gw // RET_HEADS
    sb_hd = gw // SB_HEADS
    n_exp = w_router.shape[1]
    alpha = (2 * depth) ** 0.25
    moe_rows = min(256, m)
    comb_rows = min(128, m)
    fox_blk = min(512, seq)

    xf = x.reshape(m, d_model)
    xb = xf.astype(BF16)
    p3 = p.reshape(depth, m, p.shape[-1])
    w_gate_b, w_up_b, w_down_b = w_gate.astype(BF16), w_up.astype(BF16), w_down.astype(BF16)
    w_router_pad = jnp.zeros((d_model, LANE), F32).at[:, :n_exp].set(w_router).astype(BF16)
    b_router_pad = jnp.full((1, LANE), NEG, F32).at[0, :n_exp].set(b_router)

    inv = ROPE_BASE ** (-jnp.arange(0, ret_hd, 2, dtype=F32) / ret_hd)
    log_g = jnp.log1p(-jnp.exp2(-5.0 - jnp.arange(RET_HEADS, dtype=F32)))
    cos, sin = _rope_tables(positions.reshape(m, 1).astype(F32), jnp.repeat(inv, 2).reshape(1, ret_hd))

    n_main = 2 * ch + 3 * gw
    even_scale = jnp.ones((n_main,), F32).at[2 * ch:2 * ch + gw].set(fox_hd ** -0.5)
    odd_scale = jnp.ones((7 * gw,), F32).at[:gw].set(ret_hd ** -0.5).at[4 * gw:5 * gw].set(sb_hd ** -0.5)

    for i in range(depth):
        j = i // 2
        if i % 2 == 0:
            h = _proj_in(*_quant_rows(xb), w_in_even, j, n_main, even_scale)
            w_f = jnp.zeros((d_model, LANE), F32).at[:, :fox_heads].set(w_in_even[j, :, n_main:]).astype(BF16)
            f_logit = _matmul(xb, w_f, F32)[:, :fox_heads]
            f_rows = f_logit.reshape(batch, seq, fox_heads).transpose(0, 2, 1).reshape(batch * fox_heads, seq)
            bias_rows = jnp.tile(fox_f_bias[j], batch).reshape(batch * fox_heads, 1)
            cum = _fox_cum(f_rows, bias_rows).reshape(batch * fox_heads, seq // fox_blk, 1, fox_blk)
            conv_out = _conv_module(h, batch, seq, ch, conv_w[j], conv_b[j], conv_ln_g[j], conv_ln_b[j])
            q_col = 2 * ch // fox_hd
            fox_out = _fox_attention(h, cum, batch, seq, fox_heads, fox_hd,
                                     q_col, q_col + fox_heads, q_col + 2 * fox_heads, blk=fox_blk)
            mixed = _proj_out(*conv_out, *_quant_rows(fox_out), w_out_even, j)
        else:
            h = _proj_in(*_quant_rows(xb), w_in_odd, j, 7 * gw, odd_scale)
            ret_out = _retention(h, cos, sin, log_g, ret_norm_g[j], batch, seq, RET_HEADS, ret_hd)
            s_col = 4 * gw // sb_hd
            sb_out = _sb_attention(h, batch, seq, SB_HEADS, sb_hd, s_col, s_col + SB_HEADS, s_col + 2 * SB_HEADS)
            mixed = _proj_out(*_quant_rows(ret_out), *_quant_rows(sb_out), w_out_odd, j)
        xf, xb = _res_ln(xf, mixed, ln_mix_g[i], ln_mix_b[i], alpha)
        route = _router(xb, w_router_pad, b_router_pad, n_exp)
        src_rows, tile_expert, tile_valid, slots = _routing_tables(route, n_exp, moe_rows, comb_rows)
        y = _moe_ffn(xf, src_rows, tile_expert, tile_valid, w_gate_b, w_up_b, w_down_b, i, moe_rows)
        xf, x8, x8_scale = _moe_combine_ln(xf, y, route, slots, ln_ffn_g[i], ln_ffn_b[i], alpha, tm=comb_rows)
        xf, xb = _ple(x8, x8_scale, xf, p3, w_ple_gate, w_ple, i)
    return xf.reshape(batch, seq, d_model)
```

```python
import functools

import jax
import jax.numpy as jnp
from jax import lax
from jax.experimental import pallas as pl
from jax.experimental.pallas import tpu as pltpu

F32 = jnp.float32
BF16 = jnp.bfloat16
FP8 = jnp.float8_e4m3fn
FP8_PEAK = 256.0

CHUNK = 64
N_GROUPS = 4
RET_HEADS = 8
SB_HEADS = 16
ROPE_BASE = 10000.0
LN_EPS = 1e-5
NEG = -1e30

LANE = 128
SUBLANE = 8
VMEM_LIMIT_BYTES = 56 * 1024 * 1024


def _cparams(n_axes):
    return pltpu.CompilerParams(dimension_semantics=("arbitrary",) * n_axes,
                                vmem_limit_bytes=VMEM_LIMIT_BYTES)


def _sigmoid(x):
    return 1.0 / (1.0 + jnp.exp(-x))


def _dot(a, b):
    return jnp.dot(a, b, preferred_element_type=F32)


def _dot_nt(a, b):
    return lax.dot_general(a, b, (((1,), (1,)), ((), ())), preferred_element_type=F32)


def _mm_kernel(a_ref, b_ref, o_ref):
    o_ref[...] = _dot(a_ref[...], b_ref[...]).astype(o_ref.dtype)


def _tile(n, pref):
    t = min(pref, n)
    while n % t or (t % LANE and t != n):
        t -= LANE if t % LANE == 0 else t % LANE
    return t


def _matmul(a, b, out_dtype, tm=1024, tn=512):
    m, kd = a.shape
    n = b.shape[1]
    tm, tn = _tile(m, tm), _tile(n, tn)
    return pl.pallas_call(
        _mm_kernel,
        grid=(m // tm, n // tn),
        in_specs=[pl.BlockSpec((tm, kd), lambda i, j: (i, 0)),
                  pl.BlockSpec((kd, tn), lambda i, j: (0, j))],
        out_specs=pl.BlockSpec((tm, tn), lambda i, j: (i, j)),
        out_shape=jax.ShapeDtypeStruct((m, n), out_dtype),
        compiler_params=_cparams(2),
        name="matmul",
    )(a, b)


def _quant_rows_store(x, q_ref, s_ref):
    amax = jnp.max(jnp.abs(x), axis=-1, keepdims=True)
    scale = jnp.where(amax > 0.0, amax * (1.0 / FP8_PEAK), 1.0)
    q_ref[...] = (x / scale).astype(FP8)
    s_ref[...] = jnp.broadcast_to(scale, s_ref.shape)


def _quant_rows_kernel(x_ref, q_ref, s_ref):
    _quant_rows_store(x_ref[...].astype(F32), q_ref, s_ref)


def _quant_rows(x, tm=512):
    m, kd = x.shape
    tm = _tile(m, tm)
    return pl.pallas_call(
        _quant_rows_kernel,
        grid=(m // tm,),
        in_specs=[pl.BlockSpec((tm, kd), lambda i: (i, 0))],
        out_specs=[pl.BlockSpec((tm, kd), lambda i: (i, 0)), pl.BlockSpec((tm, LANE), lambda i: (i, 0))],
        out_shape=[jax.ShapeDtypeStruct((m, kd), FP8), jax.ShapeDtypeStruct((m, LANE), F32)],
        compiler_params=_cparams(1),
        name="quant_rows",
    )(x)


def _quant_cols(w_ref, col_mult, wq_ref, ws_ref):
    kd, tn = wq_ref.shape
    chunk = _tile(kd, 512)

    def amax_body(c, acc):
        w = w_ref[0, pl.ds(pl.multiple_of(c * chunk, chunk), chunk), :]
        return jnp.maximum(acc, jnp.max(jnp.abs(w), axis=0, keepdims=True))

    amax = lax.fori_loop(0, kd // chunk, amax_body, jnp.zeros((1, tn), F32)) * jnp.abs(col_mult)
    scale = jnp.where(amax > 0.0, amax * (1.0 / FP8_PEAK), 1.0)
    mult = col_mult / scale

    def quant_body(c, carry):
        rows = pl.ds(pl.multiple_of(c * chunk, chunk), chunk)
        wq_ref[rows, :] = (w_ref[0, rows, :] * mult).astype(FP8)
        return carry

    lax.fori_loop(0, kd // chunk, quant_body, 0)
    ws_ref[...] = jnp.broadcast_to(scale, ws_ref.shape)


def _proj_in_kernel(a_ref, r_ref, w_ref, s_ref, o_ref, wq_ref, ws_ref):
    @pl.when(pl.program_id(1) == 0)
    def _():
        _quant_cols(w_ref, s_ref[...], wq_ref, ws_ref)

    o_ref[...] = (_dot(a_ref[...], wq_ref[...]) * r_ref[:, 0:1] * ws_ref[0:1, :]).astype(o_ref.dtype)


def _proj_in(a8, a_scale, w_all, layer, n_cols, col_scale, tm=1024, tn=1024):
    m, kd = a8.shape
    tm, tn = _tile(m, tm), _tile(n_cols, tn)
    return pl.pallas_call(
        _proj_in_kernel,
        grid=(n_cols // tn, m // tm),
        in_specs=[pl.BlockSpec((tm, kd), lambda n, i: (i, 0)),
                  pl.BlockSpec((tm, LANE), lambda n, i: (i, 0)),
                  pl.BlockSpec((1, kd, tn), lambda n, i: (layer, 0, n)),
                  pl.BlockSpec((1, tn), lambda n, i: (0, n))],
        out_specs=pl.BlockSpec((tm, tn), lambda n, i: (i, n)),
        out_shape=jax.ShapeDtypeStruct((m, n_cols), BF16),
        scratch_shapes=[pltpu.VMEM((kd, tn), FP8), pltpu.VMEM((SUBLANE, tn), F32)],
        compiler_params=_cparams(2),
        name="proj_in",
    )(a8, a_scale, w_all, col_scale.reshape(1, n_cols))


def _proj_out_kernel(a1_ref, r1_ref, a2_ref, r2_ref, w1_ref, w2_ref, o_ref, wq1_ref, ws1_ref, wq2_ref, ws2_ref):
    @pl.when(pl.program_id(1) == 0)
    def _():
        _quant_cols(w1_ref, jnp.ones((1, wq1_ref.shape[1]), F32), wq1_ref, ws1_ref)
        _quant_cols(w2_ref, jnp.ones((1, wq2_ref.shape[1]), F32), wq2_ref, ws2_ref)

    o_ref[...] = (_dot(a1_ref[...], wq1_ref[...]) * r1_ref[:, 0:1] * ws1_ref[0:1, :]
                  + _dot(a2_ref[...], wq2_ref[...]) * r2_ref[:, 0:1] * ws2_ref[0:1, :]).astype(o_ref.dtype)


def _proj_out(a1, r1, a2, r2, w_all, layer, tm=1024, tn=1024):
    m, k1 = a1.shape
    n = w_all.shape[2]
    assert a2.shape[1] == k1 and w_all.shape[1] == 2 * k1
    tm, tn = _tile(m, tm), _tile(n, tn)
    act = pl.BlockSpec((tm, k1), lambda c, i: (i, 0))
    rsc = pl.BlockSpec((tm, LANE), lambda c, i: (i, 0))
    return pl.pallas_call(
        _proj_out_kernel,
        grid=(n // tn, m // tm),
        in_specs=[act, rsc, act, rsc,
                  pl.BlockSpec((1, k1, tn), lambda c, i: (layer, 0, c)),
                  pl.BlockSpec((1, k1, tn), lambda c, i: (layer, 1, c))],
        out_specs=pl.BlockSpec((tm, tn), lambda c, i: (i, c)),
        out_shape=jax.ShapeDtypeStruct((m, n), BF16),
        scratch_shapes=[pltpu.VMEM((k1, tn), FP8), pltpu.VMEM((SUBLANE, tn), F32),
                        pltpu.VMEM((k1, tn), FP8), pltpu.VMEM((SUBLANE, tn), F32)],
        compiler_params=_cparams(2),
        name="proj_out",
    )(a1, r1, a2, r2, w_all, w_all)


def _layernorm_rows(t, g, b):
    mu = jnp.mean(t, axis=-1, keepdims=True)
    d = t - mu
    var = jnp.mean(d * d, axis=-1, keepdims=True)
    return d * lax.rsqrt(var + LN_EPS) * g + b


def _res_ln_kernel(x_ref, y_ref, g_ref, b_ref, of_ref, ob_ref, *, alpha):
    o = _layernorm_rows(alpha * x_ref[...] + y_ref[...].astype(F32), g_ref[...], b_ref[...])
    of_ref[...] = o
    ob_ref[...] = o.astype(BF16)


def _res_ln(x, y, g, b, alpha, tm=256):
    m, d = x.shape
    tm = min(tm, m)
    row = pl.BlockSpec((tm, d), lambda i: (i, 0))
    vec = pl.BlockSpec((1, d), lambda i: (0, 0))
    return pl.pallas_call(
        functools.partial(_res_ln_kernel, alpha=alpha),
        grid=(m // tm,),
        in_specs=[row, row, vec, vec],
        out_specs=[row, row],
        out_shape=[jax.ShapeDtypeStruct((m, d), F32), jax.ShapeDtypeStruct((m, d), BF16)],
        compiler_params=_cparams(1),
        name="res_ln",
    )(x, y, g.reshape(1, d), b.reshape(1, d))


def _conv_kernel(a_ref, g_ref, ap_ref, gp_ref, cw_ref, cb_ref, lg_ref, lb_ref, o_ref, os_ref, buf_ref, y_ref,
                 *, t_rows, halo, width):
    i = pl.program_id(1)
    ch = a_ref.shape[1]
    buf_ref[pl.ds(halo, t_rows), :] = a_ref[...].astype(F32) * _sigmoid(g_ref[...].astype(F32))
    ap = ap_ref[pl.ds(t_rows - halo, halo), :].astype(F32)
    gp = gp_ref[pl.ds(t_rows - halo, halo), :].astype(F32)
    tail = ap * _sigmoid(gp)
    buf_ref[pl.ds(0, halo), :] = jnp.where(i > 0, tail, 0.0)
    base = halo - (width - 1)
    for c0 in range(0, ch, LANE):
        acc = jnp.zeros((t_rows, LANE), F32) + cb_ref[:, c0:c0 + LANE]
        for r in range(SUBLANE):
            taps = [w for w in range(width) if (base + w) % SUBLANE == r]
            if not taps:
                continue
            rows = t_rows + (SUBLANE if r else 0)
            z = None
            for w in taps:
                term = buf_ref[pl.ds(base + w - r, rows), c0:c0 + LANE] * cw_ref[w:w + 1, c0:c0 + LANE]
                z = term if z is None else z + term
            acc = acc + z[r:r + t_rows, :]
        y_ref[:, c0:c0 + LANE] = acc
    y = _layernorm_rows(y_ref[...], lg_ref[...], lb_ref[...])
    _quant_rows_store(y * _sigmoid(y), o_ref, os_ref)


def _conv_module(h, batch, seq, ch, conv_w, conv_b, ln_g, ln_b, t_rows=256, halo=32):
    width = conv_w.shape[0]
    t_rows = min(t_rows, seq)
    assert width - 1 <= halo <= t_rows and seq % t_rows == 0 and halo % SUBLANE == 0
    nt = seq // t_rows
    cur = lambda col: pl.BlockSpec((t_rows, ch), lambda b, i: (b * nt + i, col))
    prev = lambda col: pl.BlockSpec((t_rows, ch), lambda b, i: (b * nt + jnp.maximum(i - 1, 0), col))
    vec = pl.BlockSpec((1, ch), lambda b, i: (0, 0))
    return pl.pallas_call(
        functools.partial(_conv_kernel, t_rows=t_rows, halo=halo, width=width),
        grid=(batch, nt),
        in_specs=[cur(0), cur(1), prev(0), prev(1),
                  pl.BlockSpec((width, ch), lambda b, i: (0, 0)), vec, vec, vec],
        out_specs=[pl.BlockSpec((t_rows, ch), lambda b, i: (b * nt + i, 0)),
                   pl.BlockSpec((t_rows, LANE), lambda b, i: (b * nt + i, 0))],
        out_shape=[jax.ShapeDtypeStruct((batch * seq, ch), FP8), jax.ShapeDtypeStruct((batch * seq, LANE), F32)],
        scratch_shapes=[pltpu.VMEM((t_rows + halo, ch), F32), pltpu.VMEM((t_rows, ch), F32)],
        compiler_params=_cparams(2),
        name="conv_module",
    )(h, h, h, h, conv_w, conv_b.reshape(1, ch), ln_g.reshape(1, ch), ln_b.reshape(1, ch))


def _cum_kernel(f_ref, b_ref, o_ref):
    x = f_ref[...] + b_ref[...]
    ls = jnp.minimum(x, 0.0) - jnp.log(1.0 + jnp.exp(-jnp.abs(x)))
    lane = lax.broadcasted_iota(jnp.int32, ls.shape, 1)
    sh = 1
    while sh < ls.shape[1]:
        ls = ls + jnp.where(lane >= sh, pltpu.roll(ls, sh, 1), 0.0)
        sh *= 2
    o_ref[...] = ls


def _fox_cum(f_rows, bias_rows):
    r, s = f_rows.shape
    return pl.pallas_call(
        _cum_kernel,
        grid=(1,),
        in_specs=[pl.BlockSpec((r, s), lambda i: (0, 0)), pl.BlockSpec((r, 1), lambda i: (0, 0))],
        out_specs=pl.BlockSpec((r, s), lambda i: (0, 0)),
        out_shape=jax.ShapeDtypeStruct((r, s), F32),
        compiler_params=_cparams(1),
        name="fox_cum",
    )(f_rows, bias_rows)


def _fox_kernel(q_ref, k_ref, v_ref, c_ref, o_ref, *, blk):
    i = pl.program_id(2)
    hd = q_ref.shape[1]
    q = q_ref[...]
    row = lax.broadcasted_iota(jnp.int32, (blk, blk), 0)
    col = lax.broadcasted_iota(jnp.int32, (blk, blk), 1)

    def step(kj, carry, masked):
        m, l, acc = carry
        start = pl.multiple_of(kj * blk, blk)
        k = k_ref[pl.ds(start, blk), :]
        v = v_ref[pl.ds(start, blk), :]
        s = _dot_nt(q, k) - c_ref[0, kj]
        if masked:
            s = jnp.where(col <= row, s, NEG)
        m_new = jnp.maximum(m, jnp.max(s, axis=-1, keepdims=True))
        p = jnp.exp(s - m_new)
        alpha = jnp.exp(m - m_new)
        l = alpha * l + jnp.sum(p, axis=-1, keepdims=True)
        acc = alpha * acc + _dot(p.astype(BF16), v)
        return m_new, l, acc

    init = (jnp.full((blk, 1), NEG, F32), jnp.zeros((blk, 1), F32), jnp.zeros((blk, hd), F32))
    carry = lax.fori_loop(0, i, lambda kj, c: step(kj, c, False), init)
    _, l, acc = step(i, carry, True)
    o_ref[...] = (acc / l).astype(o_ref.dtype)


def _fox_attention(h, cum, batch, seq, n_heads, hd, q_col, k_col, v_col, blk=512):
    blk = min(blk, seq)
    nq = seq // blk
    return pl.pallas_call(
        functools.partial(_fox_kernel, blk=blk),
        grid=(batch, n_heads, nq),
        in_specs=[pl.BlockSpec((blk, hd), lambda b, hh, i: (b * nq + i, q_col + hh)),
                  pl.BlockSpec((seq, hd), lambda b, hh, i: (b, k_col + hh)),
                  pl.BlockSpec((seq, hd), lambda b, hh, i: (b, v_col + hh)),
                  pl.BlockSpec((1, nq, 1, blk), lambda b, hh, i: (b * n_heads + hh, 0, 0, 0))],
        out_specs=pl.BlockSpec((blk, hd), lambda b, hh, i: (b * nq + i, hh)),
        out_shape=jax.ShapeDtypeStruct((batch * seq, n_heads * hd), BF16),
        compiler_params=_cparams(3),
        name="fox_attention",
    )(h, h, h, cum)


SB_DEAD = 105.0


def _sb_kernel(q_ref, k_ref, v_ref, o_ref, *, blk, heads, hd):
    i = pl.program_id(2)
    jr = lax.broadcasted_iota(jnp.int32, (blk, blk), 0)
    jc = lax.broadcasted_iota(jnp.int32, (blk, blk), 1)
    upper = jnp.where(jr > jc, 1.0, 0.0).astype(BF16)
    strict = jc < jr

    def block(hh, start, run, acc, masked):
        cols = slice(hh * hd, (hh + 1) * hd)
        z = _dot_nt(q_ref[:, cols], k_ref[pl.ds(start, blk), cols])
        sp = jnp.maximum(z, 0.0) + jnp.log(1.0 + jnp.exp(-jnp.abs(z)))
        log_1m = -sp
        log_b = z - sp
        if masked:
            log_1m = jnp.where(strict, log_1m, 0.0)
        hi = log_1m.astype(BF16)
        lo = (log_1m - hi.astype(F32)).astype(BF16)
        later = _dot(hi, upper) + _dot(lo, upper)
        w = jnp.exp(log_b + later + run)
        if masked:
            w = jnp.where(strict, w, 0.0)
        acc = acc + _dot(w.astype(BF16), v_ref[pl.ds(start, blk), cols])
        return run + later[:, 0:1] + log_1m[:, 0:1], acc

    def step(kj, carry, masked):
        start = pl.multiple_of(kj * blk, blk)
        return tuple(block(hh, start, *carry[hh], masked) for hh in range(heads))

    def live(carry):
        return functools.reduce(jnp.maximum, [jnp.max(run) for run, _ in carry])

    carry = tuple((jnp.zeros((blk, 1), F32), jnp.zeros((blk, hd), F32)) for _ in range(heads))
    carry = step(i, carry, True)

    def cond(state):
        t, top, _ = state
        return jnp.logical_and(t < i, top > -SB_DEAD)

    def body(state):
        t, _, c = state
        c = step(i - 1 - t, c, False)
        return t + 1, live(c), c

    _, _, carry = lax.while_loop(cond, body, (jnp.int32(0), live(carry), carry))
    for hh in range(heads):
        o_ref[:, hh * hd:(hh + 1) * hd] = carry[hh][1].astype(o_ref.dtype)


def _sb_attention(h, batch, seq, n_heads, hd, q_col, k_col, v_col, blk=256, heads=2):
    blk = min(blk, seq)
    nq = seq // blk
    assert n_heads % heads == 0 and q_col % heads == 0 and k_col % heads == 0 and v_col % heads == 0
    wide = heads * hd
    return pl.pallas_call(
        functools.partial(_sb_kernel, blk=blk, heads=heads, hd=hd),
        grid=(batch, n_heads // heads, nq),
        in_specs=[pl.BlockSpec((blk, wide), lambda b, hh, i: (b * nq + i, q_col // heads + hh)),
                  pl.BlockSpec((seq, wide), lambda b, hh, i: (b, k_col // heads + hh)),
                  pl.BlockSpec((seq, wide), lambda b, hh, i: (b, v_col // heads + hh))],
        out_specs=pl.BlockSpec((blk, wide), lambda b, hh, i: (b * nq + i, hh)),
        out_shape=jax.ShapeDtypeStruct((batch * seq, n_heads * hd), BF16),
        compiler_params=_cparams(3),
        name="sb_attention",
    )(h, h, h)


def _rope_kernel(pos_ref, inv_ref, cos_ref, sin_ref):
    ang = pos_ref[...] * inv_ref[...]
    lane = lax.broadcasted_iota(jnp.int32, ang.shape, 1)
    cos_ref[...] = jnp.cos(ang)
    sin_ref[...] = jnp.where((lane & 1) == 0, -jnp.sin(ang), jnp.sin(ang))


def _rope_tables(pos_col, inv_row, tm=512):
    m = pos_col.shape[0]
    half = inv_row.shape[1]
    tm = min(tm, m)
    out = pl.BlockSpec((tm, half), lambda i: (i, 0))
    return pl.pallas_call(
        _rope_kernel,
        grid=(m // tm,),
        in_specs=[pl.BlockSpec((tm, 1), lambda i: (i, 0)), pl.BlockSpec((1, half), lambda i: (0, 0))],
        out_specs=[out, out],
        out_shape=[jax.ShapeDtypeStruct((m, half), F32)] * 2,
        compiler_params=_cparams(1),
        name="rope_tables",
    )(pos_col, inv_row)


def _ret_kernel(lg_ref, q_ref, k_ref, v_ref, gate_ref, cos_ref, sin_ref, gn_ref, o_ref, state_ref, *, blk):
    hh = pl.program_id(1)
    i = pl.program_id(2)
    d = q_ref.shape[1]
    lg = lg_ref[hh]

    @pl.when(i == 0)
    def _():
        state_ref[...] = jnp.zeros_like(state_ref)

    cos = cos_ref[...]
    sin = sin_ref[...]
    even = (lax.broadcasted_iota(jnp.int32, (blk, d), 1) & 1) == 0

    def rot(t_ref):
        t = t_ref[...].astype(F32)
        partner = jnp.where(even, pltpu.roll(t, d - 1, 1), pltpu.roll(t, 1, 1))
        return t * cos + partner * sin

    qr = rot(q_ref)
    kr = rot(k_ref)
    v = v_ref[...]
    n = lax.broadcasted_iota(jnp.int32, (blk, 1), 0).astype(F32)
    q_decay = jnp.exp((n + 1.0) * lg)
    k_decay = jnp.exp((blk - 1.0 - n) * lg)
    row = lax.broadcasted_iota(jnp.int32, (blk, blk), 0)
    col = lax.broadcasted_iota(jnp.int32, (blk, blk), 1)
    dist = jnp.abs(row - col).astype(F32)
    shift = CHUNK.bit_length() - 1
    seen = (col >> shift) <= (row >> shift)
    decay = jnp.where(seen, jnp.exp(dist * lg), 0.0)
    scores = _dot_nt(qr.astype(BF16), kr.astype(BF16)) * decay
    out = _dot(scores.astype(BF16), v) + _dot((qr * q_decay).astype(BF16), state_ref[...].astype(BF16))
    kv = lax.dot_general((kr * k_decay).astype(BF16), v, (((0,), (0,)), ((), ())), preferred_element_type=F32)
    state_ref[...] = jnp.exp(jnp.full((1, d), blk * lg, F32)) * state_ref[...] + kv
    mu = jnp.mean(out, axis=-1, keepdims=True)
    dv = out - mu
    var = jnp.mean(dv * dv, axis=-1, keepdims=True)
    y = dv * lax.rsqrt(var + LN_EPS) * gn_ref[...]
    g = gate_ref[...].astype(F32)
    o_ref[...] = (y * (g * _sigmoid(g))).astype(o_ref.dtype)


def _retention(h, cos, sin, log_g, gn, batch, seq, n_heads, d, blk=256):
    blk = min(blk, seq)
    assert blk % CHUNK == 0 and CHUNK & (CHUNK - 1) == 0
    nb = seq // blk
    hcol = lambda off: pl.BlockSpec((blk, d), lambda b, hh, i: (b * nb + i, off + hh))
    tab = pl.BlockSpec((blk, d), lambda b, hh, i: (b * nb + i, 0))
    return pl.pallas_call(
        functools.partial(_ret_kernel, blk=blk),
        grid=(batch, n_heads, nb),
        in_specs=[pl.BlockSpec(memory_space=pltpu.SMEM),
                  hcol(0), hcol(n_heads), hcol(2 * n_heads), hcol(3 * n_heads), tab, tab,
                  pl.BlockSpec((1, d), lambda b, hh, i: (0, hh))],
        out_specs=pl.BlockSpec((blk, d), lambda b, hh, i: (b * nb + i, hh)),
        out_shape=jax.ShapeDtypeStruct((batch * seq, n_heads * d), BF16),
        scratch_shapes=[pltpu.VMEM((d, d), F32)],
        compiler_params=_cparams(3),
        name="retention",
    )(log_g, h, h, h, h, cos, sin, gn.reshape(1, n_heads * d))


def _router_kernel(x_ref, w_ref, b_ref, o_ref, *, n_exp, epg):
    logits = _dot(x_ref[...], w_ref[...]) + b_ref[...]
    lane = lax.broadcasted_iota(jnp.int32, logits.shape, 1)
    lmax = jnp.max(logits, axis=-1, keepdims=True)
    p = jnp.where(lane < n_exp, jnp.exp(logits - lmax), -1.0)
    p1 = jnp.max(p, axis=-1, keepdims=True)
    e1 = jnp.min(jnp.where(p == p1, lane, LANE), axis=-1, keepdims=True)
    shift = epg.bit_length() - 1
    in_group = (lane >> shift) == (e1 >> shift)
    cand = jnp.where(in_group, jnp.where(lane == e1, -1.0, p), -1.0)
    p2 = jnp.max(cand, axis=-1, keepdims=True)
    e2 = jnp.min(jnp.where(cand == p2, lane, LANE), axis=-1, keepdims=True)
    tot = p1 + p2
    o_ref[...] = jnp.where(lane == 0, e1.astype(F32),
                           jnp.where(lane == 1, e2.astype(F32),
                                     jnp.where(lane == 2, p1 / tot, jnp.where(lane == 3, p2 / tot, 0.0))))


def _router(xb, w_pad, b_pad, n_exp, tm=512):
    m, d = xb.shape
    tm = min(tm, m)
    epg = n_exp // N_GROUPS
    assert epg & (epg - 1) == 0 and n_exp <= LANE
    return pl.pallas_call(
        functools.partial(_router_kernel, n_exp=n_exp, epg=epg),
        grid=(m // tm,),
        in_specs=[pl.BlockSpec((tm, d), lambda i: (i, 0)),
                  pl.BlockSpec((d, LANE), lambda i: (0, 0)),
                  pl.BlockSpec((1, LANE), lambda i: (0, 0))],
        out_specs=pl.BlockSpec((tm, LANE), lambda i: (i, 0)),
        out_shape=jax.ShapeDtypeStruct((m, LANE), F32),
        compiler_params=_cparams(1),
        name="router",
    )(xb, w_pad, b_pad)


GATHER_UNROLL = 8


def _row_copy(src_hbm, row, dst_ref, r, sem):
    return pltpu.make_async_copy(src_hbm.at[pl.ds(row, 1), :], dst_ref.at[pl.ds(r, 1), :], sem)


def _gather_start(idx_ref, n_rows, src_hbm, dst_ref, sem):
    def issue(r, carry):
        _row_copy(src_hbm, idx_ref[0, 0, r], dst_ref, r, sem).start()
        return carry
    lax.fori_loop(0, n_rows, issue, 0, unroll=GATHER_UNROLL)


def _gather_wait(n_rows, src_hbm, dst_ref, sem):
    def drain(r, carry):
        _row_copy(src_hbm, 0, dst_ref, r, sem).wait()
        return carry
    lax.fori_loop(0, n_rows, drain, 0, unroll=GATHER_UNROLL)


def _moe_kernel(te_ref, tv_ref, cur_ref, nxt_ref, x_hbm, wg_ref, wu_ref, wd_ref, y_ref, xbuf, sem, *, t_rows):
    t = pl.program_id(0)
    slot = t % 2

    @pl.when(t == 0)
    def _():
        _gather_start(cur_ref, t_rows, x_hbm, xbuf.at[0], sem.at[0])

    @pl.when(jnp.logical_and(t + 1 < pl.num_programs(0), tv_ref[jnp.minimum(t + 1, pl.num_programs(0) - 1)] == 1))
    def _():
        _gather_start(nxt_ref, t_rows, x_hbm, xbuf.at[1 - slot], sem.at[1 - slot])

    @pl.when(tv_ref[t] == 1)
    def _():
        _gather_wait(t_rows, x_hbm, xbuf.at[slot], sem.at[slot])
        xb = xbuf[slot].astype(BF16)
        hg = _dot(xb, wg_ref[0, 0])
        hu = _dot(xb, wu_ref[0, 0])
        hidden = (hg * _sigmoid(hg) * hu).astype(BF16)
        y_ref[...] = _dot(hidden, wd_ref[0, 0])

    @pl.when(tv_ref[t] == 0)
    def _():
        y_ref[...] = jnp.zeros_like(y_ref)


def _moe_ffn(x, src_rows, tile_expert, tile_valid, w_gate, w_up, w_down, layer, t_rows):
    n_tiles = tile_expert.shape[0]
    d = x.shape[1]
    f = w_gate.shape[3]
    wspec = lambda a, b: pl.BlockSpec((1, 1, a, b), lambda t, te, tv: (layer, te[t], 0, 0))
    grid_spec = pltpu.PrefetchScalarGridSpec(
        num_scalar_prefetch=2,
        grid=(n_tiles,),
        in_specs=[pl.BlockSpec((1, 1, t_rows), lambda t, te, tv: (t, 0, 0), memory_space=pltpu.SMEM),
                  pl.BlockSpec((1, 1, t_rows), lambda t, te, tv: (jnp.minimum(t + 1, n_tiles - 1), 0, 0),
                               memory_space=pltpu.SMEM),
                  pl.BlockSpec(memory_space=pl.ANY),
                  wspec(d, f), wspec(d, f), wspec(f, d)],
        out_specs=pl.BlockSpec((t_rows, d), lambda t, te, tv: (t, 0)),
        scratch_shapes=[pltpu.VMEM((2, t_rows, d), F32), pltpu.SemaphoreType.DMA((2,))],
    )
    src3 = src_rows.reshape(n_tiles, 1, t_rows)
    return pl.pallas_call(
        functools.partial(_moe_kernel, t_rows=t_rows),
        grid_spec=grid_spec,
        out_shape=jax.ShapeDtypeStruct((n_tiles * t_rows, d), F32),
        compiler_params=_cparams(1),
        name="moe_ffn",
    )(tile_expert, tile_valid, src3, src3, x, w_gate, w_up, w_down)


def _combine_kernel(cur_ref, nxt_ref, route_ref, x_ref, y_hbm, g_ref, b_ref, of_ref, oq_ref, os_ref, buf, sem,
                    *, tm, alpha):
    i = pl.program_id(0)
    slot = i % 2

    @pl.when(i == 0)
    def _():
        _gather_start(cur_ref, 2 * tm, y_hbm, buf.at[0], sem.at[0])

    @pl.when(i + 1 < pl.num_programs(0))
    def _():
        _gather_start(nxt_ref, 2 * tm, y_hbm, buf.at[1 - slot], sem.at[1 - slot])

    _gather_wait(2 * tm, y_hbm, buf.at[slot], sem.at[slot])
    w1 = route_ref[:, 2:3]
    w2 = route_ref[:, 3:4]
    ffn = w1 * buf[slot, pl.ds(0, tm), :] + w2 * buf[slot, pl.ds(tm, tm), :]
    o = _layernorm_rows(alpha * x_ref[...] + ffn, g_ref[...], b_ref[...])
    of_ref[...] = o
    _quant_rows_store(o, oq_ref, os_ref)


def _moe_combine_ln(x, y, route, slots, g, b, alpha, tm=128):
    m, d = x.shape
    row = pl.BlockSpec((tm, d), lambda i: (i, 0))
    vec = pl.BlockSpec((1, d), lambda i: (0, 0))
    return pl.pallas_call(
        functools.partial(_combine_kernel, tm=tm, alpha=alpha),
        grid=(m // tm,),
        in_specs=[pl.BlockSpec((1, 1, 2 * tm), lambda i: (i, 0, 0), memory_space=pltpu.SMEM),
                  pl.BlockSpec((1, 1, 2 * tm), lambda i: (jnp.minimum(i + 1, m // tm - 1), 0, 0),
                               memory_space=pltpu.SMEM),
                  pl.BlockSpec((tm, LANE), lambda i: (i, 0)),
                  row,
                  pl.BlockSpec(memory_space=pl.ANY),
                  vec, vec],
        out_specs=[row, row, pl.BlockSpec((tm, LANE), lambda i: (i, 0))],
        out_shape=[jax.ShapeDtypeStruct((m, d), F32), jax.ShapeDtypeStruct((m, d), FP8),
                   jax.ShapeDtypeStruct((m, LANE), F32)],
        scratch_shapes=[pltpu.VMEM((2, 2 * tm, d), F32), pltpu.SemaphoreType.DMA((2,))],
        compiler_params=_cparams(1),
        name="moe_combine_ln",
    )(slots, slots, route, x, y, g.reshape(1, d), b.reshape(1, d))


def _slot_kernel(route_ref, slot_ref, ends_ref, cnt_ref, run_ref, start_ref, *, tb, t_rows):
    ph = pl.program_id(0)
    i = pl.program_id(1)
    r = route_ref[...]
    lane = lax.broadcasted_iota(jnp.int32, (tb, LANE), 1)
    oh1 = lane == r[:, 0:1].astype(jnp.int32)
    oh2 = lane == r[:, 1:2].astype(jnp.int32)
    both = jnp.where(oh1, 1.0, 0.0) + jnp.where(oh2, 1.0, 0.0)
    col_sum = jnp.sum(both, axis=0, keepdims=True)

    @pl.when(jnp.logical_and(ph == 0, i == 0))
    def _():
        cnt_ref[...] = jnp.zeros_like(cnt_ref)

    @pl.when(ph == 0)
    def _():
        cnt_ref[...] = cnt_ref[...] + col_sum

    @pl.when(jnp.logical_and(ph == 1, i == 0))
    def _():
        shift = t_rows.bit_length() - 1
        padded = ((cnt_ref[...].astype(jnp.int32) + (t_rows - 1)) >> shift) << shift
        lane8 = lax.broadcasted_iota(jnp.int32, padded.shape, 1)
        ends = padded
        sh = 1
        while sh < LANE:
            ends = ends + jnp.where(lane8 >= sh, pltpu.roll(ends, sh, 1), 0)
            sh *= 2
        start_ref[...] = (ends - padded).astype(F32)
        ends_ref[...] = ends.astype(F32)
        run_ref[...] = jnp.zeros_like(run_ref)

    @pl.when(ph == 1)
    def _():
        tr = lax.broadcasted_iota(jnp.int32, (tb, tb), 0)
        tc = lax.broadcasted_iota(jnp.int32, (tb, tb), 1)
        earlier = jnp.where(tr > tc, 1.0, 0.0).astype(BF16)
        before = _dot(earlier, both.astype(BF16)) + run_ref[0:1, :] + start_ref[0:1, :]
        s1 = jnp.sum(jnp.where(oh1, before, 0.0), axis=-1, keepdims=True)
        s2 = jnp.sum(jnp.where(oh2, before, 0.0), axis=-1, keepdims=True)
        slot_ref[...] = jnp.where(lane == 0, s1, jnp.where(lane == 1, s2, 0.0))
        run_ref[...] = run_ref[...] + col_sum


def _moe_slots(route, t_rows, tb=512):
    m = route.shape[0]
    tb = min(tb, m)
    assert t_rows & (t_rows - 1) == 0
    acc = pltpu.VMEM((8, LANE), F32)
    return pl.pallas_call(
        functools.partial(_slot_kernel, tb=tb, t_rows=t_rows),
        grid=(2, m // tb),
        in_specs=[pl.BlockSpec((tb, LANE), lambda ph, i: (i, 0))],
        out_specs=[pl.BlockSpec((tb, LANE), lambda ph, i: (i * ph, 0)),
                   pl.BlockSpec((8, LANE), lambda ph, i: (0, 0))],
        out_shape=[jax.ShapeDtypeStruct((m, LANE), F32), jax.ShapeDtypeStruct((8, LANE), F32)],
        scratch_shapes=[acc, acc, acc],
        compiler_params=_cparams(2),
        name="moe_slots",
    )(route)


def _routing_tables(route, n_exp, t_rows, tm):
    m = route.shape[0]
    slots_f, ends_f = _moe_slots(route, t_rows)
    slot2 = slots_f[:, 0:2].astype(jnp.int32)
    ends = ends_f[0, :n_exp].astype(jnp.int32)
    n_slots = 2 * m + n_exp * t_rows
    n_tiles = n_slots // t_rows
    src_rows = jnp.zeros((n_slots,), jnp.int32).at[slot2.reshape(-1)].set(jnp.arange(2 * m, dtype=jnp.int32) // 2)
    tile_start = jnp.arange(n_tiles, dtype=jnp.int32) * t_rows
    tile_valid = (tile_start < ends[-1]).astype(jnp.int32)
    last_valid = jnp.maximum(ends[-1] - 1, 0)
    probe = jnp.minimum(tile_start, last_valid)
    tile_expert = jnp.sum((ends[None, :] <= probe[:, None]).astype(jnp.int32), axis=1)
    tile_expert = jnp.minimum(tile_expert, n_exp - 1)
    slots = jnp.concatenate([slot2[:, 0].reshape(m // tm, 1, tm), slot2[:, 1].reshape(m // tm, 1, tm)], axis=2)
    return src_rows, tile_expert, tile_valid, slots


def _ple_kernel(a_ref, r_ref, w_ref, x_ref, p_ref, wp_ref, of_ref, ob_ref, wq_ref, ws_ref):
    @pl.when(pl.program_id(1) == 0)
    def _():
        _quant_cols(w_ref, jnp.ones((1, wq_ref.shape[1]), F32), wq_ref, ws_ref)

    gate = _sigmoid(_dot(a_ref[...], wq_ref[...]) * r_ref[:, 0:1] * ws_ref[0:1, :])
    o = x_ref[...] + gate * _dot(p_ref[0].astype(BF16), wp_ref[0].astype(BF16))
    of_ref[...] = o
    ob_ref[...] = o.astype(BF16)


def _ple(a8, a_scale, xf, p_all, w_gate_all, w_ple_all, layer, tm=1024, tn=512):
    m, d = a8.shape
    pd = p_all.shape[2]
    tm, tn = _tile(m, tm), _tile(d, tn)
    out = pl.BlockSpec((tm, tn), lambda c, i: (i, c))
    return pl.pallas_call(
        _ple_kernel,
        grid=(d // tn, m // tm),
        in_specs=[pl.BlockSpec((tm, d), lambda c, i: (i, 0)),
                  pl.BlockSpec((tm, LANE), lambda c, i: (i, 0)),
                  pl.BlockSpec((1, d, tn), lambda c, i: (layer, 0, c)),
                  out,
                  pl.BlockSpec((1, tm, pd), lambda c, i: (layer, i, 0)),
                  pl.BlockSpec((1, pd, tn), lambda c, i: (layer, 0, c))],
        out_specs=[out, out],
        out_shape=[jax.ShapeDtypeStruct((m, d), F32), jax.ShapeDtypeStruct((m, d), BF16)],
        scratch_shapes=[pltpu.VMEM((d, tn), FP8), pltpu.VMEM((SUBLANE, tn), F32)],
        compiler_params=_cparams(2),
        name="ple",
    )(a8, a_scale, w_gate_all, xf, p_all, w_ple_all)


def kernel(x, p, positions, w_in_even, conv_w, conv_b, conv_ln_g, conv_ln_b, fox_f_bias, w_out_even, w_in_odd, ret_norm_g, w_out_odd, ln_mix_g, ln_mix_b, ln_ffn_g, ln_ffn_b, w_router, b_router, w_gate, w_up, w_down, w_ple, w_ple_gate):
    batch, seq, d_model = x.shape
    depth = ln_mix_g.shape[0]
    m = batch * seq
    gw = d_model // 2
    ch = conv_w.shape[2]
    fox_heads = fox_f_bias.shape[1]
    fox_hd = gw // fox_heads
    ret_hd = gw // RET_HEADS
    sb_hd = gw // SB_HEADS
    n_exp = w_router.shape[1]
    alpha = (2 * depth) ** 0.25
    moe_rows = min(256, m)
    comb_rows = min(128, m)
    fox_blk = min(512, seq)

    xf = x.reshape(m, d_model)
    xb = xf.astype(BF16)
    p3 = p.reshape(depth, m, p.shape[-1])
    w_gate_b, w_up_b, w_down_b = w_gate.astype(BF16), w_up.astype(BF16), w_down.astype(BF16)
    w_router_pad = jnp.zeros((d_model, LANE), F32).at[:, :n_exp].set(w_router).astype(BF16)
    b_router_pad = jnp.full((1, LANE), NEG, F32).at[0, :n_exp].set(b_router)

    inv = ROPE_BASE ** (-jnp.arange(0, ret_hd, 2, dtype=F32) / ret_hd)
    log_g = jnp.log1p(-jnp.exp2(-5.0 - jnp.arange(RET_HEADS, dtype=F32)))
    cos, sin = _rope_tables(positions.reshape(m, 1).astype(F32), jnp.repeat(inv, 2).reshape(1, ret_hd))

    n_main = 2 * ch + 3 * gw
    even_scale = jnp.ones((n_main,), F32).at[2 * ch:2 * ch + gw].set(fox_hd ** -0.5)
    odd_scale = jnp.ones((7 * gw,), F32).at[:gw].set(ret_hd ** -0.5).at[4 * gw:5 * gw].set(sb_hd ** -0.5)

    for i in range(depth):
        j = i // 2
        if i % 2 == 0:
            h = _proj_in(*_quant_rows(xb), w_in_even, j, n_main, even_scale)
            w_f = jnp.zeros((d_model, LANE), F32).at[:, :fox_heads].set(w_in_even[j, :, n_main:]).astype(BF16)
            f_logit = _matmul(xb, w_f, F32)[:, :fox_heads]
            f_rows = f_logit.reshape(batch, seq, fox_heads).transpose(0, 2, 1).reshape(batch * fox_heads, seq)
            bias_rows = jnp.tile(fox_f_bias[j], batch).reshape(batch * fox_heads, 1)
            cum = _fox_cum(f_rows, bias_rows).reshape(batch * fox_heads, seq // fox_blk, 1, fox_blk)
            conv_out = _conv_module(h, batch, seq, ch, conv_w[j], conv_b[j], conv_ln_g[j], conv_ln_b[j])
            q_col = 2 * ch // fox_hd
            fox_out = _fox_attention(h, cum, batch, seq, fox_heads, fox_hd,
                                     q_col, q_col + fox_heads, q_col + 2 * fox_heads, blk=fox_blk)
            mixed = _proj_out(*conv_out, *_quant_rows(fox_out), w_out_even, j)
        else:
            h = _proj_in(*_quant_rows(xb), w_in_odd, j, 7 * gw, odd_scale)
            ret_out = _retention(h, cos, sin, log_g, ret_norm_g[j], batch, seq, RET_HEADS, ret_hd)
            s_col = 4 * gw // sb_hd
            sb_out = _sb_attention(h, batch, seq, SB_HEADS, sb_hd, s_col, s_col + SB_HEADS, s_col + 2 * SB_HEADS)
            mixed = _proj_out(*_quant_rows(ret_out), *_quant_rows(sb_out), w_out_odd, j)
        xf, xb = _res_ln(xf, mixed, ln_mix_g[i], ln_mix_b[i], alpha)
        route = _router(xb, w_router_pad, b_router_pad, n_exp)
        src_rows, tile_expert, tile_valid, slots = _routing_tables(route, n_exp, moe_rows, comb_rows)
        y = _moe_ffn(xf, src_rows, tile_expert, tile_valid, w_gate_b, w_up_b, w_down_b, i, moe_rows)
        xf, x8, x8_scale = _moe_combine_ln(xf, y, route, slots, ln_ffn_g[i], ln_ffn_b[i], alpha, tm=comb_rows)
        xf, xb = _ple(x8, x8_scale, xf, p3, w_ple_gate, w_ple, i)
    return xf.reshape(batch, seq, d_model)
```
